```python
import jax, jax.numpy as jnp
from jax import lax
import numpy as np

D_MODEL = 2048
BATCH = 4
SEQ = 2048
DEPTH = 4
DEC_BATCH = 128
DEC_SEQ = 8
PAST_LEN = 16384
PAGE_SIZE = 128

H_A = 6
DK_A = 64
DV_A = 128
W_A = H_A * DV_A
GLA_LOWRANK = 16
GLA_TAU = 16.0
GLA_CHUNK = 64
G_B = 4
CG_B = 128
W_B = G_B * CG_B
CHUNK_B = 128
H_C = 6
DK_C = 64
DV_C = 128
W_C = H_C * DV_C
RET_CHUNK = 128
ROPE_BASE = 10000.0
D_MIX = W_A + W_B + W_C
IN_SPLITS = (H_A * DK_A, H_A * DK_A, W_A, W_A, GLA_LOWRANK,
             W_B, W_B,
             H_C * DK_C, H_C * DK_C, W_C, W_C)
D_IN = 2 * H_A * DK_A + 2 * W_A + GLA_LOWRANK + 2 * W_B + 2 * H_C * DK_C + 2 * W_C
N_EXPERTS = 32
TOP_K = 4
D_FF = D_MODEL
SWIGLU_LIMIT = 7.0
SWIGLU_ALPHA = 1.702
MOE_BLOCK = 128
N_MOD = 6
EPS = 1e-6

kernel_name = "hymba_gla_gmlp_retnet_moe_step"

F32 = jnp.float32


def rms_norm(x, g):
    xf = x.astype(F32)
    y = xf * lax.rsqrt(jnp.mean(xf * xf, -1, keepdims=True) + EPS)
    return (y * g.astype(F32)).astype(x.dtype)


def head_rms_norm(o, g):
    return o * lax.rsqrt(jnp.mean(o * o, -1, keepdims=True) + EPS) * g.astype(F32)


def head_layer_norm(o, g):
    mu = jnp.mean(o, -1, keepdims=True)
    d = o - mu
    var = jnp.mean(d * d, -1, keepdims=True)
    return d * lax.rsqrt(var + EPS) * g.astype(F32).reshape(o.shape[2], o.shape[3])


def rotary(x, pos):
    half = x.shape[-1] // 2
    inv = ROPE_BASE ** (-jnp.arange(half, dtype=F32) / half)
    ang = pos[:, None] * inv[None]
    cos = jnp.cos(ang)[None, :, None, :]
    sin = jnp.sin(ang)[None, :, None, :]
    x1, x2 = x[..., :half], x[..., half:]
    return jnp.concatenate([x1 * cos - x2 * sin, x1 * sin + x2 * cos], -1)


def gla_chunked(q, k, v, log_a, s0):
    B, T, H, DK = q.shape
    L = GLA_CHUNK if T % GLA_CHUNK == 0 else T
    N = T // L

    def to_chunks(a):
        return jnp.moveaxis(a.astype(F32).reshape(B, N, L, *a.shape[2:]), 1, 0)

    causal = jnp.tril(jnp.ones((L, L), bool))

    def step(S, inp):
        qc, kc, vc, ac = inp
        b = jnp.cumsum(ac, axis=1)
        o_inter = jnp.einsum('bthk,bhkv->bthv', qc * jnp.exp(b), S)
        rel = jnp.exp(jnp.minimum(b[:, :, None] - b[:, None, :], 0.0))
        att = jnp.einsum('bthk,bshk,btshk->bths', qc, kc, rel)
        att = jnp.where(causal[None, :, None, :], att, 0.0)
        o_intra = jnp.einsum('bths,bshv->bthv', att, vc)
        b_end = b[:, -1]
        S_new = S * jnp.exp(b_end)[..., None] + jnp.einsum(
            'bshk,bshv->bhkv', kc * jnp.exp(b_end[:, None] - b), vc)
        return S_new, o_inter + o_intra

    S, o = lax.scan(step, s0.astype(F32), (to_chunks(q), to_chunks(k), to_chunks(v), to_chunks(log_a)))
    return jnp.moveaxis(o, 0, 1).reshape(B, T, H, v.shape[-1]), S


def retention_chunked(q, k, v, s0):
    B, T, H, DK = q.shape
    L = RET_CHUNK if T % RET_CHUNK == 0 else T
    N = T // L
    log_g = jnp.log1p(-jnp.exp2(-5.0 - jnp.arange(H, dtype=F32)))
    n = jnp.arange(L, dtype=F32)
    diff = n[:, None] - n[None, :]
    decay = jnp.where((diff >= 0)[:, None, :],
                      jnp.exp(jnp.maximum(diff, 0.0)[:, None, :] * log_g[None, :, None]), 0.0)
    q_decay = jnp.exp((n[:, None] + 1.0) * log_g[None])
    k_decay = jnp.exp((L - 1.0 - n)[:, None] * log_g[None])
    chunk_decay = jnp.exp(L * log_g)

    def to_chunks(a):
        return jnp.moveaxis(a.astype(F32).reshape(B, N, L, *a.shape[2:]), 1, 0)

    def step(S, inp):
        qc, kc, vc = inp
        o_inter = jnp.einsum('bthk,bhkv->bthv', qc, S) * q_decay[None, :, :, None]
        att = jnp.einsum('bthk,bshk->bths', qc, kc) * decay[None]
        o_intra = jnp.einsum('bths,bshv->bthv', att, vc)
        S_new = S * chunk_decay[None, :, None, None] + jnp.einsum(
            'bshk,bshv->bhkv', kc * k_decay[None, :, :, None], vc)
        return S_new, o_inter + o_intra

    S, o = lax.scan(step, s0.astype(F32), (to_chunks(q), to_chunks(k), to_chunks(v)))
    return jnp.moveaxis(o, 0, 1).reshape(B, T, H, v.shape[-1]), S


def chunk_spatial_gating(u, v, ln_g, ln_b, w_s, b_s):
    B, T = u.shape[:2]
    vf = v.astype(F32)
    mu = jnp.mean(vf, -1, keepdims=True)
    d = vf - mu
    vn = d * lax.rsqrt(jnp.mean(d * d, -1, keepdims=True) + EPS) * ln_g.astype(F32) + ln_b.astype(F32)
    Tp = -(-T // CHUNK_B) * CHUNK_B
    vp = jnp.pad(vn, ((0, 0), (0, Tp - T), (0, 0), (0, 0)))
    vc = vp.reshape(B, Tp // CHUNK_B, CHUNK_B, G_B, CG_B)
    w = jnp.where(jnp.tril(jnp.ones((CHUNK_B, CHUNK_B), bool))[None], w_s.astype(F32), 0.0)
    z = jnp.einsum('gij,bnjgc->bnigc', w, vc) + b_s.astype(F32).T[None, None, :, :, None]
    z = z.reshape(B, Tp, G_B, CG_B)[:, :T]
    return u.astype(F32) * z, vn


def moe_clamped_swiglu(x2, w_router, b_router, w_e_in, b_e_in, w_e_out, b_e_out):
    M, D = x2.shape
    logits = x2.astype(F32) @ w_router.astype(F32) + b_router.astype(F32)
    top_val, top_idx = lax.top_k(logits, TOP_K)
    gates = jax.nn.softmax(top_val, axis=-1)
    A = M * TOP_K
    flat_e = top_idx.reshape(A).astype(jnp.int32)
    flat_tok = jnp.arange(A, dtype=jnp.int32) // TOP_K
    flat_gate = gates.reshape(A)
    sorted_e, order = lax.sort((flat_e, jnp.arange(A, dtype=jnp.int32)), num_keys=1, is_stable=True)
    counts = jnp.bincount(flat_e, length=N_EXPERTS)
    padded = (counts + MOE_BLOCK - 1) // MOE_BLOCK * MOE_BLOCK
    pad_end = jnp.cumsum(padded)
    pad_start = pad_end - padded
    start = jnp.cumsum(counts) - counts
    dest = pad_start[sorted_e] + jnp.arange(A, dtype=jnp.int32) - start[sorted_e]
    n_blocks = -(-A // MOE_BLOCK) + N_EXPERTS
    P = n_blocks * MOE_BLOCK
    row_tok = jnp.full((P,), M, jnp.int32).at[dest].set(flat_tok[order])
    row_gate = jnp.zeros((P,), F32).at[dest].set(flat_gate[order])
    block_e = jnp.minimum(jnp.searchsorted(pad_end, jnp.arange(n_blocks) * MOE_BLOCK, side='right'),
                          N_EXPERTS - 1).astype(jnp.int32)
    x_rows = jnp.concatenate([x2, jnp.zeros((1, D), x2.dtype)], 0)[row_tok].reshape(n_blocks, MOE_BLOCK, D)

    def expert_block(args):
        xb, e = args
        h = xb @ w_e_in[e] + b_e_in[e]
        g, up = h[:, :D_FF], h[:, D_FF:]
        g = jnp.minimum(g, SWIGLU_LIMIT)
        up = jnp.clip(up, -SWIGLU_LIMIT, SWIGLU_LIMIT)
        glu = g * jax.nn.sigmoid(SWIGLU_ALPHA * g)
        return ((up + 1.0) * glu) @ w_e_out[e] + b_e_out[e]

    y_rows = lax.map(expert_block, (x_rows, block_e))
    y = jax.ops.segment_sum(y_rows.reshape(P, D).astype(F32) * row_gate[:, None], row_tok,
                            num_segments=M + 1)
    return y[:M].astype(x2.dtype)


def trunk(x, c, pos, s_gla, s_ret, W, return_v):
    B, T, _ = x.shape
    split_pts = np.cumsum(IN_SPLITS)[:-1].tolist()
    new_gla, new_ret, new_v = [], [], []
    for l in range(DEPTH):
        mod = (jax.nn.silu(c) @ W['w_ada'][l] + W['b_ada'][l]).reshape(B, N_MOD, D_MODEL)
        shift1, scale1, gate1, shift2, scale2, gate2 = [mod[:, i][:, None] for i in range(N_MOD)]
        h = rms_norm(x, W['g_norm1'][l]) * (1.0 + scale1) + shift1
        p = h @ W['w_in'][l]
        qa, ka, va, ga, ra, ub, vb, qc, kc, vc, gc = jnp.split(p, split_pts, axis=-1)
        log_a = jax.nn.log_sigmoid((ra @ W['w_gla_gate'][l] + W['b_gla_gate'][l]).astype(F32)) / GLA_TAU
        o_a, sa = gla_chunked(qa.reshape(B, T, H_A, DK_A).astype(F32) * DK_A ** -0.5,
                              ka.reshape(B, T, H_A, DK_A), va.reshape(B, T, H_A, DV_A),
                              log_a.reshape(B, T, H_A, DK_A), s_gla[l])
        o_a = head_rms_norm(o_a, W['g_gla_out'][l]) * jax.nn.silu(ga.reshape(B, T, H_A, DV_A).astype(F32))
        o_b, vn = chunk_spatial_gating(jax.nn.gelu(ub).reshape(B, T, G_B, CG_B),
                                       jax.nn.gelu(vb).reshape(B, T, G_B, CG_B),
                                       W['ln_g_cmlp'][l].reshape(G_B, CG_B),
                                       W['ln_b_cmlp'][l].reshape(G_B, CG_B),
                                       W['w_spatial'][l], W['b_spatial'][l])
        qr = rotary(qc.reshape(B, T, H_C, DK_C).astype(F32), pos)
        kr = rotary(kc.reshape(B, T, H_C, DK_C).astype(F32), pos) * DK_C ** -0.5
        o_c, sc = retention_chunked(qr, kr, vc.reshape(B, T, H_C, DV_C), s_ret[l])
        o_c = head_layer_norm(o_c, W['g_ret_out'][l]) * jax.nn.silu(gc.reshape(B, T, H_C, DV_C).astype(F32))
        mix = jnp.concatenate([o_a.reshape(B, T, W_A), o_b.reshape(B, T, W_B), o_c.reshape(B, T, W_C)],
                              -1).astype(x.dtype)
        x = x + gate1 * (mix @ W['w_out'][l])
        h2 = rms_norm(x, W['g_norm2'][l]) * (1.0 + scale2) + shift2
        y2 = moe_clamped_swiglu(h2.reshape(B * T, D_MODEL), W['w_router'][l], W['b_router'][l],
                                W['w_e_in'][l], W['b_e_in'][l], W['w_e_out'][l], W['b_e_out'][l])
        x = x + gate2 * y2.reshape(B, T, D_MODEL)
        new_gla.append(sa.astype(s_gla.dtype))
        new_ret.append(sc.astype(s_ret.dtype))
        if return_v:
            new_v.append(vn.reshape(B, T, W_B).astype(x.dtype))
    y = rms_norm(x, W['g_final'])
    v_out = jnp.stack(new_v) if return_v else None
    return y, jnp.stack(new_gla), jnp.stack(new_ret), v_out


def setup_inputs(seed: int = 0) -> dict:
    key = jax.random.key(seed)
    ks = iter(jax.random.split(key, 32))

    def nrm(shape, scale):
        return jax.random.normal(next(ks), shape, F32) * scale

    return {
        "x_prompt": nrm((BATCH, SEQ, D_MODEL), 1.0),
        "x_sample": nrm((DEC_BATCH, DEC_SEQ, D_MODEL), 1.0),
        "c_prompt": nrm((BATCH, D_MODEL), 1.0),
        "c_sample": nrm((DEC_BATCH, D_MODEL), 1.0),
        "state_gla": nrm((DEPTH, DEC_BATCH, H_A, DK_A, DV_A), 4.0),
        "state_ret": nrm((DEPTH, DEC_BATCH, H_C, DK_C, DV_C), 2.0),
        "w_ada": nrm((DEPTH, D_MODEL, N_MOD * D_MODEL), 0.5 * D_MODEL ** -0.5),
        "b_ada": nrm((DEPTH, N_MOD * D_MODEL), 0.02),
        "g_norm1": 1.0 + nrm((DEPTH, D_MODEL), 0.02),
        "g_norm2": 1.0 + nrm((DEPTH, D_MODEL), 0.02),
        "w_in": nrm((DEPTH, D_MODEL, D_IN), D_MODEL ** -0.5),
        "w_gla_gate": nrm((DEPTH, GLA_LOWRANK, H_A * DK_A), GLA_LOWRANK ** -0.5),
        "b_gla_gate": nrm((DEPTH, H_A * DK_A), 0.1),
        "g_gla_out": 1.0 + nrm((DEPTH, DV_A), 0.02),
        "ln_g_cmlp": 1.0 + nrm((DEPTH, W_B), 0.02),
        "ln_b_cmlp": nrm((DEPTH, W_B), 0.02),
        "w_spatial": nrm((DEPTH, G_B, CHUNK_B, CHUNK_B), CHUNK_B ** -0.5),
        "b_spatial": 1.0 + nrm((DEPTH, G_B, CHUNK_B), 0.1),
        "g_ret_out": 1.0 + nrm((DEPTH, W_C), 0.02),
        "w_out": nrm((DEPTH, D_MIX, D_MODEL), D_MIX ** -0.5),
        "w_router": nrm((DEPTH, D_MODEL, N_EXPERTS), D_MODEL ** -0.5),
        "b_router": nrm((DEPTH, N_EXPERTS), 0.01),
        "w_e_in": nrm((DEPTH, N_EXPERTS, D_MODEL, 2 * D_FF), D_MODEL ** -0.5),
        "b_e_in": nrm((DEPTH, N_EXPERTS, 2 * D_FF), 0.02),
        "w_e_out": nrm((DEPTH, N_EXPERTS, D_FF, D_MODEL), D_FF ** -0.5),
        "b_e_out": nrm((DEPTH, N_EXPERTS, D_MODEL), 0.02),
        "g_final": 1.0 + nrm((D_MODEL,), 0.02),
    }


def reference(x_prompt, x_sample, c_prompt, c_sample, state_gla, state_ret, w_ada, b_ada, g_norm1,
              g_norm2, w_in, w_gla_gate, b_gla_gate, g_gla_out, ln_g_cmlp, ln_b_cmlp, w_spatial,
              b_spatial, g_ret_out, w_out, w_router, b_router, w_e_in, b_e_in, w_e_out, b_e_out, g_final):
    W = {"w_ada": w_ada, "b_ada": b_ada, "g_norm1": g_norm1, "g_norm2": g_norm2, "w_in": w_in,
         "w_gla_gate": w_gla_gate, "b_gla_gate": b_gla_gate, "g_gla_out": g_gla_out,
         "ln_g_cmlp": ln_g_cmlp, "ln_b_cmlp": ln_b_cmlp, "w_spatial": w_spatial,
         "b_spatial": b_spatial, "g_ret_out": g_ret_out, "w_out": w_out, "w_router": w_router,
         "b_router": b_router, "w_e_in": w_e_in, "b_e_in": b_e_in, "w_e_out": w_e_out,
         "b_e_out": b_e_out, "g_final": g_final}
    Bp, Tp = x_prompt.shape[0], x_prompt.shape[1]
    zeros_gla = jnp.zeros((DEPTH, Bp, H_A, DK_A, DV_A), x_prompt.dtype)
    zeros_ret = jnp.zeros((DEPTH, Bp, H_C, DK_C, DV_C), x_prompt.dtype)
    pos_p = jnp.arange(Tp, dtype=F32)
    y_prompt, gla_p, ret_p, _ = trunk(x_prompt, c_prompt, pos_p, zeros_gla, zeros_ret, W, False)
    pos_s = jnp.arange(x_sample.shape[1], dtype=F32) + PAST_LEN
    y_sample, gla_s, ret_s, v_s = trunk(x_sample, c_sample, pos_s, state_gla, state_ret, W, True)
    return (y_prompt, y_sample, gla_p, ret_p, gla_s, ret_s, v_s)
```

```python
import functools

import numpy as np
import jax
import jax.numpy as jnp
from jax import lax
from jax.experimental import pallas as pl
from jax.experimental.pallas import tpu as pltpu

F32 = jnp.float32
BF16 = jnp.bfloat16
HIGHEST = lax.Precision.HIGHEST

D_MODEL = 2048
DEPTH = 4
N_HEADS = 6
N_PAIR = N_HEADS // 2
DK = 64
DV = 128
GLA_LOWRANK = 16
GLA_TAU = 16.0
GLA_CHUNK = 64
GLA_SUB = 16
G_B = 4
CG_B = 128
W_B = G_B * CG_B
CHUNK_B = 128
RET_CHUNK = 128
ROPE_BASE = 10000.0
N_EXPERTS = 32
TOP_K = 4
D_FF = D_MODEL
SWIGLU_LIMIT = 7.0
SWIGLU_ALPHA = 1.702
N_MOD = 6
EPS = 1e-6
PAST_LEN = 16384

LANES = 128
SUBLANES = 8
VMEM_LIMIT = 56 * 1024 * 1024

COL_UB = 0
COL_VB = 512
COL_QA = 1024
COL_KA = 1408
COL_VA = 1792
COL_GA = 2560
COL_QC = 3328
COL_KC = 3712
COL_VC = 4096
COL_GC = 4864
COL_RA = 5632
D_P = 5760
_O_RA = 2304
_O_UB = 2320
_O_QC = 3344
_O_END = 5648

EXP_CLAMP = 80.0

MOE_SUB = 256
MOE_SLAB = 1536
MOE_TN1 = 512
MOE_TN2 = 512


def _cparams(sem):
    return pltpu.CompilerParams(dimension_semantics=sem, vmem_limit_bytes=VMEM_LIMIT)


def _ada_kernel(c_ref, w_ref, b_ref, o_ref):
    c = c_ref[...]
    s = (c * jax.nn.sigmoid(c)).astype(BF16)
    o_ref[...] = jnp.dot(s, w_ref[...].astype(BF16), preferred_element_type=F32) + b_ref[...]


def _ada_call(c_all, w_ada, b_ada, tn=1024):
    nb, d = c_all.shape
    depth, _, n = w_ada.shape
    return pl.pallas_call(
        _ada_kernel,
        grid=(depth, n // tn),
        in_specs=[
            pl.BlockSpec((nb, d), lambda l, j: (0, 0)),
            pl.BlockSpec((None, d, tn), lambda l, j: (l, 0, j)),
            pl.BlockSpec((None, 1, tn), lambda l, j: (l, 0, j)),
        ],
        out_specs=pl.BlockSpec((None, nb, tn), lambda l, j: (l, 0, j)),
        out_shape=jax.ShapeDtypeStruct((depth, nb, n), F32),
        compiler_params=_cparams(("arbitrary", "arbitrary")),
        name="ada_mod",
    )(c_all, w_ada, b_ada.reshape(depth, 1, n))


def _rms_mod(x, gn, sc, sh):
    ms = jnp.mean(x * x, axis=-1, keepdims=True)
    h = x * lax.rsqrt(ms + EPS) * gn
    return h * (1.0 + sc) + sh


def _inproj_kernel(has_y, *refs):
    if has_y:
        x_ref, y_ref, g2_ref, gn_ref, sc_ref, sh_ref, w_ref, xo_ref, p_ref, h_scr = refs
    else:
        x_ref, gn_ref, sc_ref, sh_ref, w_ref, p_ref, h_scr = refs

    @pl.when(pl.program_id(1) == 0)
    def _():
        x = x_ref[...]
        if has_y:
            x = x + g2_ref[...] * y_ref[...]
            xo_ref[...] = x
        h_scr[...] = _rms_mod(x, gn_ref[...], sc_ref[...], sh_ref[...]).astype(BF16)

    p_ref[...] = jnp.dot(h_scr[...], w_ref[...], preferred_element_type=F32)


def _mod_spec(mod, tiles_per_group):
    _, r, d = mod.shape
    return pl.BlockSpec((None, r, d), lambda i, j=0: (i // tiles_per_group, 0, 0))


def _inproj_call(x, y2, gate2, gnorm, scale, shift, w, *, tm, tn, tiles_per_group):
    rows, d = x.shape
    n = w.shape[1]
    has_y = y2 is not None
    row_spec = pl.BlockSpec((tm, d), lambda i, j: (i, 0))
    in_specs, args = [row_spec], [x]
    if has_y:
        in_specs += [row_spec, _mod_spec(gate2, tiles_per_group)]
        args += [y2, gate2]
    in_specs += [pl.BlockSpec((1, d), lambda i, j: (0, 0)), _mod_spec(scale, tiles_per_group),
                 _mod_spec(shift, tiles_per_group), pl.BlockSpec((d, tn), lambda i, j: (0, j))]
    args += [gnorm, scale, shift, w]
    p_spec = pl.BlockSpec((tm, tn), lambda i, j: (i, j))
    p_shape = jax.ShapeDtypeStruct((rows, n), F32)
    if has_y:
        out_specs, out_shape = [row_spec, p_spec], [jax.ShapeDtypeStruct((rows, d), F32), p_shape]
    else:
        out_specs, out_shape = p_spec, p_shape
    out = pl.pallas_call(
        functools.partial(_inproj_kernel, has_y),
        grid=(rows // tm, n // tn),
        in_specs=in_specs, out_specs=out_specs, out_shape=out_shape,
        scratch_shapes=[pltpu.VMEM((tm, d), BF16)],
        compiler_params=_cparams(("arbitrary", "arbitrary")),
        name="norm_inproj",
    )(*args)
    return out if has_y else (x, out)


def _head_masks(width=LANES):
    lane = lax.broadcasted_iota(jnp.int32, (1, width), 1) % LANES
    return lane < DK, lane >= DK


def _expand(x, bb, ls):
    if bb == 1:
        return x
    seq = lax.broadcasted_iota(jnp.int32, (bb * ls, 1), 0) // ls
    return jnp.concatenate([jnp.where(seq == b, x, 0.0) for b in range(bb)], axis=1)


def _dot_nt(a, b):
    return lax.dot_general(a, b, (((1,), (1,)), ((), ())), preferred_element_type=F32)


def _dot_tn(a, b, precision=None):
    return lax.dot_general(a, b, (((0,), (0,)), ((), ())), preferred_element_type=F32,
                           precision=precision)


def _silu(x):
    return x * jax.nn.sigmoid(x)


def _seq_consts(bb, ls, sub):
    rows = bb * ls
    t = np.arange(rows)
    same_seq = (t[:, None] // ls) == (t[None, :] // ls)
    causal = same_seq & (t[None, :] <= t[:, None])
    sub_start = (t // sub) * sub
    in_sub = causal & (t[None, :] >= sub_start[:, None])
    return in_sub.astype(np.float32), same_seq.astype(np.float32), causal.astype(np.float32)


def _gla_body(q, k, v, g, ra, wg, bg, gout, mt, mo, cmask, s_stack, bb, ls):
    rows = bb * ls
    sub = min(GLA_SUB, ls)
    n_sub = ls // sub
    assert bb == 1 or n_sub == 1
    hm = _head_masks()

    x = jnp.dot(ra.astype(BF16), wg, preferred_element_type=F32) + bg
    log_a = (jnp.minimum(x, 0.0) - jnp.log1p(jnp.exp(-jnp.abs(x)))) * (1.0 / GLA_TAU)
    bs = jnp.dot(mt, log_a, preferred_element_type=F32, precision=HIGHEST)
    if n_sub > 1:
        bcum = jnp.dot(cmask, log_a, preferred_element_type=F32, precision=HIGHEST)
    else:
        bcum = bs
    bend = jnp.dot(mo, log_a, preferred_element_type=F32, precision=HIGHEST)

    qs = q * (DK ** -0.5)
    q_sub = qs * jnp.exp(bs)
    q_seq = qs * jnp.exp(bcum)
    k_end = k * jnp.exp(bend - bcum)

    sub_id = (lax.broadcasted_iota(jnp.int32, (rows, 1), 0) % ls) // sub
    a_parts, k_parts = [], []
    for i in range(n_sub):
        if i == 0:
            k_i = k * jnp.exp(jnp.minimum(-bcum, EXP_CLAMP))
        else:
            k_i = k * jnp.exp(jnp.minimum(bcum[i * sub - 1:i * sub, :] - bcum, EXP_CLAMP))
        k_parts.append(k_i.astype(BF16))
        a_parts.append(jnp.where(sub_id == i, q_sub, 0.0) if n_sub > 1 else q_sub)
    a = a_parts[0] if n_sub == 1 else jnp.concatenate(a_parts, axis=1)
    kk = k_parts[0] if n_sub == 1 else jnp.concatenate(k_parts, axis=1)
    hmw = _head_masks(n_sub * LANES)
    a2 = jnp.concatenate([jnp.where(hmw[0], a, 0.0), jnp.where(hmw[1], a, 0.0)], axis=0).astype(BF16)
    att = _dot_nt(a2, kk)
    att = jnp.where(jnp.concatenate([cmask, cmask], axis=0) > 0.0, att, 0.0)

    s_bf = s_stack.astype(BF16)
    outs = []
    upd = None
    for h in range(2):
        vh = v[:, h * DV:(h + 1) * DV].astype(BF16)
        o = jnp.dot(att[h * rows:(h + 1) * rows].astype(BF16), vh, preferred_element_type=F32)
        o = o + jnp.dot(_expand(jnp.where(hm[h], q_seq, 0.0), bb, ls).astype(BF16), s_bf,
                        preferred_element_type=F32)
        o = o * lax.rsqrt(jnp.mean(o * o, axis=-1, keepdims=True) + EPS) * gout
        outs.append(o * _silu(g[:, h * DV:(h + 1) * DV]))
        u = _dot_tn(_expand(jnp.where(hm[h], k_end, 0.0), bb, ls).astype(BF16), vh)
        upd = u if upd is None else upd + u
    ones = jnp.ones((rows, LANES), F32)
    dec = jnp.exp(_dot_tn(_expand(log_a, bb, ls), ones, precision=HIGHEST))
    return jnp.concatenate(outs, axis=1), s_stack * dec + upd


def _gla_kernel(cfg, *refs):
    bb, ls, nchunk, has_s0 = cfg
    (q_ref, k_ref, v_ref, g_ref, ra_ref, wg_ref, bg_ref, gout_ref, mt_ref, mo_ref, cm_ref) = refs[:11]
    if has_s0:
        s0_ref, o_ref, s_ref = refs[11:]
    else:
        o_ref, s_ref = refs[11:]

    @pl.when(pl.program_id(2) == 0)
    def _():
        s_ref[...] = s0_ref[...] if has_s0 else jnp.zeros(s_ref.shape, F32)

    rows = bb * ls
    s = s_ref[...].reshape(bb * LANES, LANES)
    for ci in range(nchunk):
        sl = slice(ci * rows, (ci + 1) * rows)
        o, s = _gla_body(q_ref[sl, :], k_ref[sl, :], v_ref[sl, :], g_ref[sl, :], ra_ref[sl, :],
                         wg_ref[...], bg_ref[...], gout_ref[...], mt_ref[...],
                         mo_ref[...], cm_ref[...], s, bb, ls)
        o_ref[sl, :] = o.astype(o_ref.dtype)
    s_ref[...] = s.reshape(s_ref.shape)


def _mixer_grid(n_seq, t, bb, ls, nchunk):
    tl = bb * ls * nchunk
    if bb == 1:
        steps = t // tl
        grid = (n_seq, N_PAIR, steps)
        row = lambda o, j, c: o * steps + c
    else:
        assert t == ls and nchunk == 1
        grid = (n_seq // bb, N_PAIR, 1)
        row = lambda o, j, c: o
    return tl, grid, row


def _gla_call(p, wg, bg, gout, s0, *, n_seq, t, bb, ls, nchunk):
    rows_total = n_seq * t
    tl, grid, row = _mixer_grid(n_seq, t, bb, ls, nchunk)
    has_s0 = s0 is not None
    mt, mo, cm = [jnp.asarray(m) for m in _seq_consts(bb, ls, min(GLA_SUB, ls))]
    r = bb * ls

    def col(width, base):
        return pl.BlockSpec((tl, width), lambda o, j, c: (row(o, j, c), base // width + j))

    const = lambda shape: pl.BlockSpec(shape, lambda o, j, c: (0,) * len(shape))
    st_spec = pl.BlockSpec((bb, None, LANES, LANES), lambda o, j, c: (o, j, 0, 0))
    in_specs = [col(LANES, COL_QA), col(LANES, COL_KA), col(2 * DV, COL_VA), col(2 * DV, COL_GA),
                pl.BlockSpec((tl, LANES), lambda o, j, c: (row(o, j, c), COL_RA // LANES)),
                pl.BlockSpec((LANES, LANES), lambda o, j, c: (0, j)),
                pl.BlockSpec((1, LANES), lambda o, j, c: (0, j)),
                const((1, DV)), const((r, r)), const((r, r)), const((r, r))]
    args = [p, p, p, p, p, wg, bg, gout, mt, mo, cm]
    if has_s0:
        in_specs.append(st_spec)
        args.append(s0)
    return pl.pallas_call(
        functools.partial(_gla_kernel, (bb, ls, nchunk, has_s0)),
        grid=grid, in_specs=in_specs,
        out_specs=[pl.BlockSpec((tl, 2 * DV), lambda o, j, c: (row(o, j, c), j)), st_spec],
        out_shape=[jax.ShapeDtypeStruct((rows_total, N_HEADS * DV), BF16),
                   jax.ShapeDtypeStruct((n_seq, N_PAIR, LANES, LANES), F32)],
        compiler_params=_cparams(("arbitrary", "arbitrary", "arbitrary")),
        name="gla_mixer",
    )(*args)


def _ret_tables(bb, ls, pos0):
    rows = bb * ls
    log_g = np.log1p(-np.exp2(-5.0 - np.arange(N_HEADS, dtype=np.float64)))
    t = np.arange(rows)
    n = t % ls
    same_seq = (t[:, None] // ls) == (t[None, :] // ls)
    diff = (n[:, None] - n[None, :]).astype(np.float64)
    causal = same_seq & (diff >= 0)
    dmask = np.where(causal[None], np.exp(np.maximum(diff, 0.0)[None] * log_g[:, None, None]), 0.0)
    lane_head = np.arange(N_PAIR * LANES) // DK
    q_dec = np.exp((n[:, None] + 1.0) * log_g[lane_head][None])
    k_dec = np.exp((ls - 1.0 - n)[:, None] * log_g[lane_head][None])
    c_dec = np.exp(ls * log_g[lane_head])[:, None] * np.ones((1, LANES))
    c_dec = np.tile(c_dec.reshape(N_PAIR, 1, LANES, LANES), (1, bb, 1, 1)).reshape(N_PAIR, bb * LANES, LANES)
    f32 = lambda a: jnp.asarray(a, dtype=F32)
    return f32(dmask), f32(q_dec), f32(k_dec), f32(c_dec)


def _rope_tables(t, pos0):
    half = DK // 2
    inv = ROPE_BASE ** (-np.arange(half, dtype=np.float64) / half)
    ang = (np.arange(t, dtype=np.float64) + pos0)[:, None] * inv[None]
    cos = np.tile(np.cos(ang), (1, 2 * LANES // DK))
    sin = np.tile(np.concatenate([-np.sin(ang), np.sin(ang)], axis=1), (1, LANES // DK))
    return jnp.asarray(cos, dtype=F32), jnp.asarray(sin, dtype=F32)


def _rope(x, cos, sin_signed):
    lane = lax.broadcasted_iota(jnp.int32, (1, LANES), 1) % DK
    swapped = jnp.where(lane < DK // 2, pltpu.roll(x, LANES - DK // 2, 1), pltpu.roll(x, DK // 2, 1))
    return x * cos + swapped * sin_signed


def _ret_body(q, k, v, g, cos, sin, dmask, q_dec, k_dec, c_dec, gn, s_stack, bb, ls):
    hm = _head_masks()
    qr = _rope(q, cos, sin)
    kr = _rope(k, cos, sin) * (DK ** -0.5)
    kr_bf = kr.astype(BF16)
    q_in = qr * q_dec
    k_end = kr * k_dec
    s_bf = s_stack.astype(BF16)
    outs = []
    upd = None
    for h in range(2):
        vh = v[:, h * DV:(h + 1) * DV].astype(BF16)
        att = _dot_nt(jnp.where(hm[h], qr, 0.0).astype(BF16), kr_bf) * dmask[h]
        o = jnp.dot(att.astype(BF16), vh, preferred_element_type=F32)
        o = o + jnp.dot(_expand(jnp.where(hm[h], q_in, 0.0), bb, ls).astype(BF16), s_bf,
                        preferred_element_type=F32)
        d = o - jnp.mean(o, axis=-1, keepdims=True)
        var = jnp.mean(d * d, axis=-1, keepdims=True)
        o = d * lax.rsqrt(var + EPS) * gn[:, h * DV:(h + 1) * DV]
        outs.append(o * _silu(g[:, h * DV:(h + 1) * DV]))
        u = _dot_tn(_expand(jnp.where(hm[h], k_end, 0.0), bb, ls).astype(BF16), vh)
        upd = u if upd is None else upd + u
    return jnp.concatenate(outs, axis=1), s_stack * c_dec + upd


def _ret_kernel(cfg, *refs):
    bb, ls, nchunk, has_s0 = cfg
    (q_ref, k_ref, v_ref, g_ref, cos_ref, sin_ref, dm_ref, qd_ref, kd_ref, cd_ref, gn_ref) = refs[:11]
    if has_s0:
        s0_ref, o_ref, s_ref = refs[11:]
    else:
        o_ref, s_ref = refs[11:]

    @pl.when(pl.program_id(2) == 0)
    def _():
        s_ref[...] = s0_ref[...] if has_s0 else jnp.zeros(s_ref.shape, F32)

    rows = bb * ls
    s = s_ref[...].reshape(bb * LANES, LANES)
    for ci in range(nchunk):
        sl = slice(ci * rows, (ci + 1) * rows)
        o, s = _ret_body(q_ref[sl, :], k_ref[sl, :], v_ref[sl, :], g_ref[sl, :], cos_ref[sl, :],
                         sin_ref[sl, :], dm_ref[...], qd_ref[...], kd_ref[...], cd_ref[...],
                         gn_ref[...], s, bb, ls)
        o_ref[sl, :] = o.astype(o_ref.dtype)
    s_ref[...] = s.reshape(s_ref.shape)


def _ret_call(p, gn, s0, *, n_seq, t, bb, ls, nchunk, pos0):
    rows_total = n_seq * t
    tl, grid, row = _mixer_grid(n_seq, t, bb, ls, nchunk)
    has_s0 = s0 is not None
    dmask, q_dec, k_dec, c_dec = _ret_tables(bb, ls, pos0)
    cos, sin = _rope_tables(t, pos0)
    r = bb * ls
    if bb > 1:
        cos, sin = jnp.tile(cos, (bb, 1)), jnp.tile(sin, (bb, 1))
        rope_spec = pl.BlockSpec((tl, LANES), lambda o, j, c: (0, 0))
    else:
        rope_spec = pl.BlockSpec((tl, LANES), lambda o, j, c: (c, 0))

    def col(width, base):
        return pl.BlockSpec((tl, width), lambda o, j, c: (row(o, j, c), base // width + j))

    st_spec = pl.BlockSpec((bb, None, LANES, LANES), lambda o, j, c: (o, j, 0, 0))
    in_specs = [col(LANES, COL_QC), col(LANES, COL_KC), col(2 * DV, COL_VC), col(2 * DV, COL_GC),
                rope_spec, rope_spec,
                pl.BlockSpec((2, r, r), lambda o, j, c: (j, 0, 0)),
                pl.BlockSpec((r, LANES), lambda o, j, c: (0, j)),
                pl.BlockSpec((r, LANES), lambda o, j, c: (0, j)),
                pl.BlockSpec((None, bb * LANES, LANES), lambda o, j, c: (j, 0, 0)),
                pl.BlockSpec((1, 2 * DV), lambda o, j, c: (0, j))]
    args = [p, p, p, p, cos, sin, dmask, q_dec, k_dec, c_dec, gn]
    if has_s0:
        in_specs.append(st_spec)
        args.append(s0)
    return pl.pallas_call(
        functools.partial(_ret_kernel, (bb, ls, nchunk, has_s0)),
        grid=grid, in_specs=in_specs,
        out_specs=[pl.BlockSpec((tl, 2 * DV), lambda o, j, c: (row(o, j, c), j)), st_spec],
        out_shape=[jax.ShapeDtypeStruct((rows_total, N_HEADS * DV), BF16),
                   jax.ShapeDtypeStruct((n_seq, N_PAIR, LANES, LANES), F32)],
        compiler_params=_cparams(("arbitrary", "arbitrary", "arbitrary")),
        name="ret_mixer",
    )(*args)


def _gelu(x):
    return 0.5 * x * (1.0 + jnp.tanh(0.7978845608028654 * (x + 0.044715 * x * x * x)))


def _cmlp_kernel(n_tiles, want_v, u_ref, v_ref, w_ref, bias_ref, lg_ref, lb_ref, *out_refs):
    o_ref = out_refs[0]
    for ti in range(n_tiles):
        sl = slice(ti * CHUNK_B, (ti + 1) * CHUNK_B)
        u = _gelu(u_ref[sl, :])
        vv = _gelu(v_ref[sl, :])
        vn_parts, z_parts = [], []
        for gi in range(G_B):
            vg = vv[:, gi * CG_B:(gi + 1) * CG_B]
            d = vg - jnp.mean(vg, axis=-1, keepdims=True)
            vn = d * lax.rsqrt(jnp.mean(d * d, axis=-1, keepdims=True) + EPS)
            vn = vn * lg_ref[:, gi * CG_B:(gi + 1) * CG_B] + lb_ref[:, gi * CG_B:(gi + 1) * CG_B]
            vn_parts.append(vn)
            z_parts.append(jnp.dot(w_ref[gi], vn.astype(BF16), preferred_element_type=F32))
        z = jnp.concatenate(z_parts, axis=1) + bias_ref[...]
        o_ref[sl, :] = (u * z).astype(o_ref.dtype)
        if want_v:
            out_refs[1][sl, :] = jnp.concatenate(vn_parts, axis=1)


def _cmlp_call(p, w_mix, bias_tile, ln_g, ln_b, *, n_tiles, want_v):
    rows = p.shape[0]
    tl = n_tiles * CHUNK_B
    const = lambda shape: pl.BlockSpec(shape, lambda i: (0,) * len(shape))
    out_specs = [pl.BlockSpec((tl, W_B), lambda i: (i, 0))]
    out_shape = [jax.ShapeDtypeStruct((rows, W_B), BF16)]
    if want_v:
        out_specs.append(pl.BlockSpec((tl, W_B), lambda i: (i, 0)))
        out_shape.append(jax.ShapeDtypeStruct((rows, W_B), F32))
    out = pl.pallas_call(
        functools.partial(_cmlp_kernel, n_tiles, want_v),
        grid=(rows // tl,),
        in_specs=[pl.BlockSpec((tl, W_B), lambda i: (i, COL_UB // W_B)),
                  pl.BlockSpec((tl, W_B), lambda i: (i, COL_VB // W_B)),
                  const((G_B, CHUNK_B, CHUNK_B)), const((CHUNK_B, W_B)), const((1, W_B)), const((1, W_B))],
        out_specs=out_specs, out_shape=out_shape,
        compiler_params=_cparams(("arbitrary",)),
        name="cmlp_mixer",
    )(p, p, w_mix, bias_tile, ln_g, ln_b)
    return (out[0], out[1]) if want_v else (out[0], None)


def _outproj_kernel(oa_ref, ob_ref, oc_ref, w_ref, x_ref, g1_ref, gn_ref, sc_ref, sh_ref,
                    wr_ref, br_ref, xo_ref, h_ref, lg_ref):
    wa = N_HEADS * DV
    acc = jnp.dot(oa_ref[...], w_ref[0:wa, :], preferred_element_type=F32)
    acc = acc + jnp.dot(ob_ref[...], w_ref[wa:wa + W_B, :], preferred_element_type=F32)
    acc = acc + jnp.dot(oc_ref[...], w_ref[wa + W_B:, :], preferred_element_type=F32)
    x = x_ref[...] + g1_ref[...] * acc
    xo_ref[...] = x
    h = _rms_mod(x, gn_ref[...], sc_ref[...], sh_ref[...])
    h_ref[...] = h.astype(h_ref.dtype)
    lg_ref[...] = jnp.dot(h, wr_ref[...], preferred_element_type=F32, precision=HIGHEST) + br_ref[...]


def _outproj_call(oa, ob, oc, w_out, x, gate1, gnorm, scale, shift, w_router, b_router, *, tm,
                  tiles_per_group):
    rows, d = x.shape
    rspec = lambda width: pl.BlockSpec((tm, width), lambda i: (i, 0))
    const = lambda shape: pl.BlockSpec(shape, lambda i: (0,) * len(shape))
    ms = lambda m: _mod_spec(m, tiles_per_group)
    return pl.pallas_call(
        _outproj_kernel,
        grid=(rows // tm,),
        in_specs=[rspec(oa.shape[1]), rspec(ob.shape[1]), rspec(oc.shape[1]), const(w_out.shape),
                  rspec(d), ms(gate1), const((1, d)), ms(scale), ms(shift),
                  const(w_router.shape), const(b_router.shape)],
        out_specs=[rspec(d), rspec(d), rspec(LANES)],
        out_shape=[jax.ShapeDtypeStruct((rows, d), F32), jax.ShapeDtypeStruct((rows, d), BF16),
                   jax.ShapeDtypeStruct((rows, LANES), F32)],
        compiler_params=_cparams(("arbitrary",)),
        name="outproj_norm_router",
    )(oa, ob, oc, w_out, x, gate1, gnorm, scale, shift, w_router, b_router)


def _moe_in_kernel(se_ref, nb_ref, sx_ref, x_ref, wg_ref, wu_ref, bg_ref, bu_ref, h_ref, wg_s, wu_s):
    nb = nb_ref[pl.program_id(0)]

    @pl.when(nb > 0)
    def _():
        wg_s[...] = wg_ref[...].astype(BF16)
        wu_s[...] = wu_ref[...].astype(BF16)
        for rb in range(MOE_SLAB // MOE_SUB):
            sl = slice(rb * MOE_SUB, (rb + 1) * MOE_SUB)

            @pl.when(rb < nb)
            def _():
                xb = x_ref[sl, :]
                gt = jnp.dot(xb, wg_s[...], preferred_element_type=F32) + bg_ref[...]
                up = jnp.dot(xb, wu_s[...], preferred_element_type=F32) + bu_ref[...]
                gt = jnp.minimum(gt, SWIGLU_LIMIT)
                up = jnp.clip(up, -SWIGLU_LIMIT, SWIGLU_LIMIT)
                glu = gt * jax.nn.sigmoid(SWIGLU_ALPHA * gt)
                h_ref[sl, :] = ((up + 1.0) * glu).astype(h_ref.dtype)

            @pl.when(rb >= nb)
            def _():
                h_ref[sl, :] = jnp.zeros((MOE_SUB, h_ref.shape[1]), h_ref.dtype)


def _moe_out_kernel(se_ref, nb_ref, sx_ref, h_ref, w_ref, b_ref, y_ref, w_s):
    nb = nb_ref[pl.program_id(0)]

    @pl.when(nb > 0)
    def _():
        w_s[...] = w_ref[...].astype(BF16)
        for rb in range(MOE_SLAB // MOE_SUB):
            sl = slice(rb * MOE_SUB, (rb + 1) * MOE_SUB)

            @pl.when(rb < nb)
            def _():
                y_ref[sl, :] = jnp.dot(h_ref[sl, :], w_s[...], preferred_element_type=F32) + b_ref[...]

            @pl.when(rb >= nb)
            def _():
                y_ref[sl, :] = jnp.zeros((MOE_SUB, y_ref.shape[1]), y_ref.dtype)


def _moe_calls(x_slabs, slab_e, slab_nb, slab_x, w_in, b_in, w_out, b_out):
    n_slab, r, d = x_slabs.shape
    nt1 = D_FF // MOE_TN1
    nt2 = d // MOE_TN2
    up_off = D_FF // MOE_TN1

    def jj(j, s, nb, last):
        return jnp.where(nb[s] > 0, j, last)

    h = pl.pallas_call(
        _moe_in_kernel,
        grid_spec=pltpu.PrefetchScalarGridSpec(
            num_scalar_prefetch=3, grid=(n_slab, nt1),
            in_specs=[
                pl.BlockSpec((None, r, d), lambda s, j, se, nb, sx: (sx[s], 0, 0)),
                pl.BlockSpec((None, d, MOE_TN1), lambda s, j, se, nb, sx: (se[s], 0, jj(j, s, nb, nt1 - 1))),
                pl.BlockSpec((None, d, MOE_TN1),
                             lambda s, j, se, nb, sx: (se[s], 0, up_off + jj(j, s, nb, nt1 - 1))),
                pl.BlockSpec((None, 1, MOE_TN1), lambda s, j, se, nb, sx: (se[s], 0, jj(j, s, nb, nt1 - 1))),
                pl.BlockSpec((None, 1, MOE_TN1),
                             lambda s, j, se, nb, sx: (se[s], 0, up_off + jj(j, s, nb, nt1 - 1))),
            ],
            out_specs=pl.BlockSpec((None, r, MOE_TN1), lambda s, j, se, nb, sx: (sx[s], 0, jj(j, s, nb, nt1 - 1))),
            scratch_shapes=[pltpu.VMEM((d, MOE_TN1), BF16), pltpu.VMEM((d, MOE_TN1), BF16)]),
        out_shape=jax.ShapeDtypeStruct((n_slab, r, D_FF), BF16),
        compiler_params=_cparams(("arbitrary", "arbitrary")),
        name="moe_glu",
    )(slab_e, slab_nb, slab_x, x_slabs, w_in, w_in, b_in, b_in)

    y = pl.pallas_call(
        _moe_out_kernel,
        grid_spec=pltpu.PrefetchScalarGridSpec(
            num_scalar_prefetch=3, grid=(n_slab, nt2),
            in_specs=[
                pl.BlockSpec((None, r, D_FF), lambda s, j, se, nb, sx: (sx[s], 0, 0)),
                pl.BlockSpec((None, D_FF, MOE_TN2), lambda s, j, se, nb, sx: (se[s], 0, jj(j, s, nb, nt2 - 1))),
                pl.BlockSpec((None, 1, MOE_TN2), lambda s, j, se, nb, sx: (se[s], 0, jj(j, s, nb, nt2 - 1))),
            ],
            out_specs=pl.BlockSpec((None, r, MOE_TN2), lambda s, j, se, nb, sx: (sx[s], 0, jj(j, s, nb, nt2 - 1))),
            scratch_shapes=[pltpu.VMEM((D_FF, MOE_TN2), BF16)]),
        out_shape=jax.ShapeDtypeStruct((n_slab, r, d), F32),
        compiler_params=_cparams(("arbitrary", "arbitrary")),
        name="moe_down",
    )(slab_e, slab_nb, slab_x, h, w_out, b_out)
    return y


def _n_slabs(n_assign):
    return N_EXPERTS + (n_assign + N_EXPERTS * (MOE_SUB - 1)) // MOE_SLAB


def _route(logits, n_slab):
    m = logits.shape[0]
    top_val, top_idx = lax.top_k(logits, TOP_K)
    gates = jax.nn.softmax(top_val, axis=-1)
    a = m * TOP_K
    flat_e = top_idx.reshape(a).astype(jnp.int32)
    ar = jnp.arange(a, dtype=jnp.int32)
    sorted_e, order = lax.sort((flat_e, ar), num_keys=1, is_stable=True)
    counts = jnp.bincount(flat_e, length=N_EXPERTS).astype(jnp.int32)
    start = jnp.cumsum(counts) - counts
    rank = ar - start[sorted_e]
    padded = (counts + MOE_SUB - 1) // MOE_SUB * MOE_SUB
    nslab_e = (padded + MOE_SLAB - 1) // MOE_SLAB
    slab_end = jnp.cumsum(nslab_e)
    slab_start = slab_end - nslab_e
    dest = (slab_start[sorted_e] + rank // MOE_SLAB) * MOE_SLAB + rank % MOE_SLAB
    pos = jnp.zeros((a,), jnp.int32).at[order].set(dest).reshape(m, TOP_K)
    row_tok = jnp.zeros((n_slab * MOE_SLAB,), jnp.int32).at[dest].set(order // TOP_K)
    n_used = slab_end[-1]
    sid = jnp.arange(n_slab, dtype=jnp.int32)
    used = sid < n_used
    sid_eff = jnp.minimum(sid, n_used - 1)
    slab_e = jnp.minimum(jnp.searchsorted(slab_end, sid_eff, side="right"), N_EXPERTS - 1).astype(jnp.int32)
    local = sid_eff - slab_start[slab_e]
    valid = jnp.clip(padded[slab_e] - local * MOE_SLAB, 0, MOE_SLAB)
    slab_nb = jnp.where(used, valid // MOE_SUB, 0).astype(jnp.int32)
    return gates, pos, row_tok, slab_e, slab_nb, sid_eff.astype(jnp.int32)


def _moe(h2, logits, w_in, b_in, w_out, b_out):
    m, d = h2.shape
    n_slab = _n_slabs(m * TOP_K)
    gates, pos, row_tok, slab_e, slab_nb, slab_x = _route(logits, n_slab)
    x_slabs = jnp.take(h2, row_tok, axis=0).reshape(n_slab, MOE_SLAB, d)
    y = _moe_calls(x_slabs, slab_e, slab_nb, slab_x, w_in, b_in.reshape(N_EXPERTS, 1, -1),
                   w_out, b_out.reshape(N_EXPERTS, 1, -1))
    y_tok = jnp.take(y.reshape(n_slab * MOE_SLAB, d), pos.reshape(-1), axis=0).reshape(m, TOP_K, d)
    return jnp.sum(y_tok * gates[:, :, None], axis=1)


def _final_kernel(x_ref, y_ref, g2_ref, gn_ref, o_ref):
    x = x_ref[...] + g2_ref[...] * y_ref[...]
    ms = jnp.mean(x * x, axis=-1, keepdims=True)
    o_ref[...] = x * lax.rsqrt(ms + EPS) * gn_ref[...]


def _final_call(x, y2, gate2, gnorm, *, tm, tiles_per_group):
    rows, d = x.shape
    rspec = pl.BlockSpec((tm, d), lambda i: (i, 0))
    return pl.pallas_call(
        _final_kernel, grid=(rows // tm,),
        in_specs=[rspec, rspec, _mod_spec(gate2, tiles_per_group), pl.BlockSpec((1, d), lambda i: (0, 0))],
        out_specs=rspec, out_shape=jax.ShapeDtypeStruct((rows, d), F32),
        compiler_params=_cparams(("arbitrary",)),
        name="final_norm",
    )(x, y2, gate2, gnorm)


def _prep_w_in(w_in):
    depth, d, _ = w_in.shape
    pad = jnp.zeros((depth, d, D_P - COL_RA - GLA_LOWRANK), w_in.dtype)
    w = jnp.concatenate([w_in[:, :, _O_UB:_O_QC], w_in[:, :, :_O_RA], w_in[:, :, _O_QC:_O_END],
                         w_in[:, :, _O_RA:_O_UB], pad], axis=-1)
    return w.astype(BF16)


def _prep_cmlp(w_spatial, b_spatial, t):
    tri = jnp.tril(jnp.ones((CHUNK_B, CHUNK_B), bool))
    w = jnp.where(tri[None, None], w_spatial, 0.0)
    if t < CHUNK_B:
        reps = CHUNK_B // t
        eye = jnp.eye(reps, dtype=w.dtype)
        w = jnp.einsum("ab,lgij->lgaibj", eye, w[:, :, :t, :t]).reshape(w.shape[0], G_B, CHUNK_B, CHUNK_B)
        b = jnp.tile(b_spatial[:, :, :t], (1, 1, reps))
    else:
        b = b_spatial
    bias = jnp.repeat(jnp.swapaxes(b, 1, 2), CG_B, axis=2)
    return w.astype(BF16), bias.astype(F32)


def kernel(x_prompt, x_sample, c_prompt, c_sample, state_gla, state_ret, w_ada, b_ada, g_norm1,
           g_norm2, w_in, w_gla_gate, b_gla_gate, g_gla_out, ln_g_cmlp, ln_b_cmlp, w_spatial,
           b_spatial, g_ret_out, w_out, w_router, b_router, w_e_in, b_e_in, w_e_out, b_e_out, g_final):
    bp, tp, d = x_prompt.shape
    bs, ts, _ = x_sample.shape
    rows_p, rows_s = bp * tp, bs * ts
    tm_p, tm_s = 512, 256
    tmo_p, tmo_s = 512, 256
    tm_f = 256
    dec_bb = LANES // ts

    n_c = bp + bs
    n_c_pad = -(-n_c // SUBLANES) * SUBLANES
    c_all = jnp.concatenate([c_prompt, c_sample, jnp.zeros((n_c_pad - n_c, d), F32)], axis=0)
    mod = _ada_call(c_all, w_ada, b_ada).reshape(DEPTH, n_c_pad, N_MOD, d)
    mod_p = jnp.transpose(mod[:, :bp], (0, 2, 1, 3)).reshape(DEPTH, N_MOD, bp, 1, d)
    mod_s = jnp.repeat(jnp.transpose(mod[:, bp:n_c], (0, 2, 1, 3)), ts, axis=2)

    w_in_r = _prep_w_in(w_in)
    w_out_bf = w_out.astype(BF16)
    wg_pad = jnp.pad(w_gla_gate, ((0, 0), (0, LANES - GLA_LOWRANK), (0, 0))).astype(BF16)
    w_router_pad = jnp.pad(w_router, ((0, 0), (0, 0), (0, LANES - N_EXPERTS)))
    b_router_pad = jnp.pad(b_router, ((0, 0), (0, LANES - N_EXPERTS))).reshape(DEPTH, 1, LANES)
    wmix_p, bias_p = _prep_cmlp(w_spatial, b_spatial, CHUNK_B)
    wmix_s, bias_s = _prep_cmlp(w_spatial, b_spatial, ts)

    xp = x_prompt.reshape(rows_p, d)
    xs = x_sample.reshape(rows_s, d)
    st_gla = state_gla.reshape(DEPTH, bs, N_PAIR, LANES, LANES)
    st_ret = state_ret.reshape(DEPTH, bs, N_PAIR, LANES, LANES)
    y2p = y2s = None
    outs = {k: [] for k in ("gla_p", "ret_p", "gla_s", "ret_s", "v_s")}

    for l in range(DEPTH):
        row = lambda a: a[l].reshape(1, -1)
        mp = lambda i: mod_p[l, i]
        msf = lambda i, tm: mod_s[l, i].reshape(rows_s // tm, tm, d)
        g2p = mod_p[l - 1, 5] if l > 0 else None
        g2s = (lambda tm: mod_s[l - 1, 5].reshape(rows_s // tm, tm, d)) if l > 0 else (lambda tm: None)

        xp, pp = _inproj_call(xp, y2p, g2p, row(g_norm1), mp(1), mp(0), w_in_r[l],
                              tm=tm_p, tn=640, tiles_per_group=tp // tm_p)
        xs, ps = _inproj_call(xs, y2s, g2s(tm_s), row(g_norm1), msf(1, tm_s), msf(0, tm_s), w_in_r[l],
                              tm=tm_s, tn=640, tiles_per_group=1)

        bg = row(b_gla_gate)
        gout = row(g_gla_out)
        gret = row(g_ret_out)
        oa_p, sa_p = _gla_call(pp, wg_pad[l], bg, gout, None, n_seq=bp, t=tp, bb=1, ls=GLA_CHUNK, nchunk=4)
        oa_s, sa_s = _gla_call(ps, wg_pad[l], bg, gout, st_gla[l], n_seq=bs, t=ts, bb=dec_bb, ls=ts, nchunk=1)
        oc_p, sc_p = _ret_call(pp, gret, None, n_seq=bp, t=tp, bb=1, ls=RET_CHUNK, nchunk=2, pos0=0)
        oc_s, sc_s = _ret_call(ps, gret, st_ret[l], n_seq=bs, t=ts, bb=dec_bb, ls=ts, nchunk=1, pos0=PAST_LEN)
        ob_p, _ = _cmlp_call(pp, wmix_p[l], bias_p[l], row(ln_g_cmlp), row(ln_b_cmlp), n_tiles=4, want_v=False)
        ob_s, v_s = _cmlp_call(ps, wmix_s[l], bias_s[l], row(ln_g_cmlp), row(ln_b_cmlp), n_tiles=4, want_v=True)

        xp, h2p, lgp = _outproj_call(oa_p, ob_p, oc_p, w_out_bf[l], xp, mp(2), row(g_norm2), mp(4), mp(3),
                                     w_router_pad[l], b_router_pad[l], tm=tmo_p, tiles_per_group=tp // tmo_p)
        xs, h2s, lgs = _outproj_call(oa_s, ob_s, oc_s, w_out_bf[l], xs, msf(2, tmo_s), row(g_norm2),
                                     msf(4, tmo_s), msf(3, tmo_s), w_router_pad[l], b_router_pad[l],
                                     tm=tmo_s, tiles_per_group=1)

        h2 = jnp.concatenate([h2p, h2s], axis=0)
        logits = jnp.concatenate([lgp, lgs], axis=0)[:, :N_EXPERTS]
        y2 = _moe(h2, logits, w_e_in[l], b_e_in[l], w_e_out[l], b_e_out[l])
        y2p, y2s = y2[:rows_p], y2[rows_p:]

        outs["gla_p"].append(sa_p.reshape(bp, N_HEADS, DK, DV))
        outs["ret_p"].append(sc_p.reshape(bp, N_HEADS, DK, DV))
        outs["gla_s"].append(sa_s.reshape(bs, N_HEADS, DK, DV))
        outs["ret_s"].append(sc_s.reshape(bs, N_HEADS, DK, DV))
        outs["v_s"].append(v_s.reshape(bs, ts, W_B))

    yp = _final_call(xp, y2p, mod_p[DEPTH - 1, 5], g_final.reshape(1, d), tm=tm_f, tiles_per_group=tp // tm_f)
    ys = _final_call(xs, y2s, mod_s[DEPTH - 1, 5].reshape(rows_s // tm_f, tm_f, d), g_final.reshape(1, d),
                     tm=tm_f, tiles_per_group=1)
    return (yp.reshape(bp, tp, d), ys.reshape(bs, ts, d), jnp.stack(outs["gla_p"]), jnp.stack(outs["ret_p"]),
            jnp.stack(outs["gla_s"]), jnp.stack(outs["ret_s"]), jnp.stack(outs["v_s"]))
```

```python
import functools

import numpy as np
import jax
import jax.numpy as jnp
from jax import lax
from jax.experimental import pallas as pl
from jax.experimental.pallas import tpu as pltpu

F32 = jnp.float32
BF16 = jnp.bfloat16
HIGHEST = lax.Precision.HIGHEST

D_MODEL = 2048
DEPTH = 4
N_HEADS = 6
N_PAIR = N_HEADS // 2
DK = 64
DV = 128
GLA_LOWRANK = 16
GLA_TAU = 16.0
GLA_CHUNK = 64
GLA_SUB = 16
G_B = 4
CG_B = 128
W_B = G_B * CG_B
CHUNK_B = 128
RET_CHUNK = 128
ROPE_BASE = 10000.0
N_EXPERTS = 32
TOP_K = 4
D_FF = D_MODEL
SWIGLU_LIMIT = 7.0
SWIGLU_ALPHA = 1.702
N_MOD = 6
EPS = 1e-6
PAST_LEN = 16384

LANES = 128
SUBLANES = 8
VMEM_LIMIT = 56 * 1024 * 1024

COL_UB = 0
COL_VB = 512
COL_QA = 1024
COL_KA = 1408
COL_VA = 1792
COL_GA = 2560
COL_QC = 3328
COL_KC = 3712
COL_VC = 4096
COL_GC = 4864
COL_RA = 5632
D_P = 5760
_O_RA = 2304
_O_UB = 2320
_O_QC = 3344
_O_END = 5648

EXP_CLAMP = 80.0

MOE_SUB = 256
MOE_SLAB = 1536
MOE_TN1 = 512
MOE_TN2 = 512


def _cparams(sem):
    return pltpu.CompilerParams(dimension_semantics=sem, vmem_limit_bytes=VMEM_LIMIT)


def _ada_kernel(c_ref, w_ref, b_ref, o_ref):
    c = c_ref[...]
    s = (c * jax.nn.sigmoid(c)).astype(BF16)
    o_ref[...] = jnp.dot(s, w_ref[...].astype(BF16), preferred_element_type=F32) + b_ref[...]


def _ada_call(c_all, w_ada, b_ada, tn=1024):
    nb, d = c_all.shape
    depth, _, n = w_ada.shape
    return pl.pallas_call(
        _ada_kernel,
        grid=(depth, n // tn),
        in_specs=[
            pl.BlockSpec((nb, d), lambda l, j: (0, 0)),
            pl.BlockSpec((None, d, tn), lambda l, j: (l, 0, j)),
            pl.BlockSpec((None, 1, tn), lambda l, j: (l, 0, j)),
        ],
        out_specs=pl.BlockSpec((None, nb, tn), lambda l, j: (l, 0, j)),
        out_shape=jax.ShapeDtypeStruct((depth, nb, n), F32),
        compiler_params=_cparams(("arbitrary", "arbitrary")),
        name="ada_mod",
    )(c_all, w_ada, b_ada.reshape(depth, 1, n))


def _rms_mod(x, gn, sc, sh):
    ms = jnp.mean(x * x, axis=-1, keepdims=True)
    h = x * lax.rsqrt(ms + EPS) * gn
    return h * (1.0 + sc) + sh


def _inproj_kernel(has_y, *refs):
    if has_y:
        x_ref, y_ref, g2_ref, gn_ref, sc_ref, sh_ref, w_ref, xo_ref, p_ref, h_scr = refs
    else:
        x_ref, gn_ref, sc_ref, sh_ref, w_ref, p_ref, h_scr = refs

    @pl.when(pl.program_id(1) == 0)
    def _():
        x = x_ref[...]
        if has_y:
            x = x + g2_ref[...] * y_ref[...]
            xo_ref[...] = x
        h_scr[...] = _rms_mod(x, gn_ref[...], sc_ref[...], sh_ref[...]).astype(BF16)

    p_ref[...] = jnp.dot(h_scr[...], w_ref[...], preferred_element_type=F32)


def _mod_spec(mod, tiles_per_group):
    _, r, d = mod.shape
    return pl.BlockSpec((None, r, d), lambda i, j=0: (i // tiles_per_group, 0, 0))


def _inproj_call(x, y2, gate2, gnorm, scale, shift, w, *, tm, tn, tiles_per_group):
    rows, d = x.shape
    n = w.shape[1]
    has_y = y2 is not None
    row_spec = pl.BlockSpec((tm, d), lambda i, j: (i, 0))
    in_specs, args = [row_spec], [x]
    if has_y:
        in_specs += [row_spec, _mod_spec(gate2, tiles_per_group)]
        args += [y2, gate2]
    in_specs += [pl.BlockSpec((1, d), lambda i, j: (0, 0)), _mod_spec(scale, tiles_per_group),
                 _mod_spec(shift, tiles_per_group), pl.BlockSpec((d, tn), lambda i, j: (0, j))]
    args += [gnorm, scale, shift, w]
    p_spec = pl.BlockSpec((tm, tn), lambda i, j: (i, j))
    p_shape = jax.ShapeDtypeStruct((rows, n), F32)
    if has_y:
        out_specs, out_shape = [row_spec, p_spec], [jax.ShapeDtypeStruct((rows, d), F32), p_shape]
    else:
        out_specs, out_shape = p_spec, p_shape
    out = pl.pallas_call(
        functools.partial(_inproj_kernel, has_y),
        grid=(rows // tm, n // tn),
        in_specs=in_specs, out_specs=out_specs, out_shape=out_shape,
        scratch_shapes=[pltpu.VMEM((tm, d), BF16)],
        compiler_params=_cparams(("arbitrary", "arbitrary")),
        name="norm_inproj",
    )(*args)
    return out if has_y else (x, out)


def _head_masks(width=LANES):
    lane = lax.broadcasted_iota(jnp.int32, (1, width), 1) % LANES
    return lane < DK, lane >= DK


def _expand(x, bb, ls):
    if bb == 1:
        return x
    seq = lax.broadcasted_iota(jnp.int32, (bb * ls, 1), 0) // ls
    return jnp.concatenate([jnp.where(seq == b, x, 0.0) for b in range(bb)], axis=1)


def _dot_nt(a, b):
    return lax.dot_general(a, b, (((1,), (1,)), ((), ())), preferred_element_type=F32)


def _dot_tn(a, b, precision=None):
    return lax.dot_general(a, b, (((0,), (0,)), ((), ())), preferred_element_type=F32,
                           precision=precision)


def _silu(x):
    return x * jax.nn.sigmoid(x)


def _seq_consts(bb, ls, sub):
    rows = bb * ls
    t = np.arange(rows)
    same_seq = (t[:, None] // ls) == (t[None, :] // ls)
    causal = same_seq & (t[None, :] <= t[:, None])
    sub_start = (t // sub) * sub
    in_sub = causal & (t[None, :] >= sub_start[:, None])
    return in_sub.astype(np.float32), same_seq.astype(np.float32), causal.astype(np.float32)


def _gla_body(q, k, v, g, ra, wg, bg, gout, mt, mo, cmask, s_stack, bb, ls):
    rows = bb * ls
    sub = min(GLA_SUB, ls)
    n_sub = ls // sub
    assert bb == 1 or n_sub == 1
    hm = _head_masks()

    x = jnp.dot(ra.astype(BF16), wg, preferred_element_type=F32) + bg
    log_a = (jnp.minimum(x, 0.0) - jnp.log1p(jnp.exp(-jnp.abs(x)))) * (1.0 / GLA_TAU)
    bs = jnp.dot(mt, log_a, preferred_element_type=F32, precision=HIGHEST)
    if n_sub > 1:
        bcum = jnp.dot(cmask, log_a, preferred_element_type=F32, precision=HIGHEST)
    else:
        bcum = bs
    bend = jnp.dot(mo, log_a, preferred_element_type=F32, precision=HIGHEST)

    qs = q * (DK ** -0.5)
    q_sub = qs * jnp.exp(bs)
    q_seq = qs * jnp.exp(bcum)
    k_end = k * jnp.exp(bend - bcum)

    sub_id = (lax.broadcasted_iota(jnp.int32, (rows, 1), 0) % ls) // sub
    a_parts, k_parts = [], []
    for i in range(n_sub):
        if i == 0:
            k_i = k * jnp.exp(jnp.minimum(-bcum, EXP_CLAMP))
        else:
            k_i = k * jnp.exp(jnp.minimum(bcum[i * sub - 1:i * sub, :] - bcum, EXP_CLAMP))
        k_parts.append(k_i.astype(BF16))
        a_parts.append(jnp.where(sub_id == i, q_sub, 0.0) if n_sub > 1 else q_sub)
    a = a_parts[0] if n_sub == 1 else jnp.concatenate(a_parts, axis=1)
    kk = k_parts[0] if n_sub == 1 else jnp.concatenate(k_parts, axis=1)
    hmw = _head_masks(n_sub * LANES)
    a2 = jnp.concatenate([jnp.where(hmw[0], a, 0.0), jnp.where(hmw[1], a, 0.0)], axis=0).astype(BF16)
    att = _dot_nt(a2, kk)
    att = jnp.where(jnp.concatenate([cmask, cmask], axis=0) > 0.0, att, 0.0)

    s_bf = s_stack.astype(BF16)
    outs = []
    upd = None
    for h in range(2):
        vh = v[:, h * DV:(h + 1) * DV].astype(BF16)
        o = jnp.dot(att[h * rows:(h + 1) * rows].astype(BF16), vh, preferred_element_type=F32)
        o = o + jnp.dot(_expand(jnp.where(hm[h], q_seq, 0.0), bb, ls).astype(BF16), s_bf,
                        preferred_element_type=F32)
        o = o * lax.rsqrt(jnp.mean(o * o, axis=-1, keepdims=True) + EPS) * gout
        outs.append(o * _silu(g[:, h * DV:(h + 1) * DV]))
        u = _dot_tn(_expand(jnp.where(hm[h], k_end, 0.0), bb, ls).astype(BF16), vh)
        upd = u if upd is None else upd + u
    ones = jnp.ones((rows, LANES), F32)
    dec = jnp.exp(_dot_tn(_expand(log_a, bb, ls), ones, precision=HIGHEST))
    return jnp.concatenate(outs, axis=1), s_stack * dec + upd


def _gla_kernel(cfg, *refs):
    bb, ls, nchunk, has_s0 = cfg
    (q_ref, k_ref, v_ref, g_ref, ra_ref, wg_ref, bg_ref, gout_ref, mt_ref, mo_ref, cm_ref) = refs[:11]
    if has_s0:
        s0_ref, o_ref, s_ref = refs[11:]
    else:
        o_ref, s_ref = refs[11:]

    @pl.when(pl.program_id(2) == 0)
    def _():
        s_ref[...] = s0_ref[...] if has_s0 else jnp.zeros(s_ref.shape, F32)

    rows = bb * ls
    s = s_ref[...].reshape(bb * LANES, LANES)
    for ci in range(nchunk):
        sl = slice(ci * rows, (ci + 1) * rows)
        o, s = _gla_body(q_ref[sl, :], k_ref[sl, :], v_ref[sl, :], g_ref[sl, :], ra_ref[sl, :],
                         wg_ref[...], bg_ref[...], gout_ref[...], mt_ref[...],
                         mo_ref[...], cm_ref[...], s, bb, ls)
        o_ref[sl, :] = o.astype(o_ref.dtype)
    s_ref[...] = s.reshape(s_ref.shape)


def _mixer_grid(n_seq, t, bb, ls, nchunk):
    tl = bb * ls * nchunk
    if bb == 1:
        steps = t // tl
        grid = (n_seq, N_PAIR, steps)
        row = lambda o, j, c: o * steps + c
    else:
        assert t == ls and nchunk == 1
        grid = (n_seq // bb, N_PAIR, 1)
        row = lambda o, j, c: o
    return tl, grid, row


def _gla_call(p, wg, bg, gout, s0, *, n_seq, t, bb, ls, nchunk):
    rows_total = n_seq * t
    tl, grid, row = _mixer_grid(n_seq, t, bb, ls, nchunk)
    has_s0 = s0 is not None
    mt, mo, cm = [jnp.asarray(m) for m in _seq_consts(bb, ls, min(GLA_SUB, ls))]
    r = bb * ls

    def col(width, base):
        return pl.BlockSpec((tl, width), lambda o, j, c: (row(o, j, c), base // width + j))

    const = lambda shape: pl.BlockSpec(shape, lambda o, j, c: (0,) * len(shape))
    st_spec = pl.BlockSpec((bb, None, LANES, LANES), lambda o, j, c: (o, j, 0, 0))
    in_specs = [col(LANES, COL_QA), col(LANES, COL_KA), col(2 * DV, COL_VA), col(2 * DV, COL_GA),
                pl.BlockSpec((tl, LANES), lambda o, j, c: (row(o, j, c), COL_RA // LANES)),
                pl.BlockSpec((LANES, LANES), lambda o, j, c: (0, j)),
                pl.BlockSpec((1, LANES), lambda o, j, c: (0, j)),
                const((1, DV)), const((r, r)), const((r, r)), const((r, r))]
    args = [p, p, p, p, p, wg, bg, gout, mt, mo, cm]
    if has_s0:
        in_specs.append(st_spec)
        args.append(s0)
    return pl.pallas_call(
        functools.partial(_gla_kernel, (bb, ls, nchunk, has_s0)),
        grid=grid, in_specs=in_specs,
        out_specs=[pl.BlockSpec((tl, 2 * DV), lambda o, j, c: (row(o, j, c), j)), st_spec],
        out_shape=[jax.ShapeDtypeStruct((rows_total, N_HEADS * DV), BF16),
                   jax.ShapeDtypeStruct((n_seq, N_PAIR, LANES, LANES), F32)],
        compiler_params=_cparams(("arbitrary", "arbitrary", "arbitrary")),
        name="gla_mixer",
    )(*args)


def _ret_tables(bb, ls, pos0):
    rows = bb * ls
    log_g = np.log1p(-np.exp2(-5.0 - np.arange(N_HEADS, dtype=np.float64)))
    t = np.arange(rows)
    n = t % ls
    same_seq = (t[:, None] // ls) == (t[None, :] // ls)
    diff = (n[:, None] - n[None, :]).astype(np.float64)
    causal = same_seq & (diff >= 0)
    dmask = np.where(causal[None], np.exp(np.maximum(diff, 0.0)[None] * log_g[:, None, None]), 0.0)
    lane_head = np.arange(N_PAIR * LANES) // DK
    q_dec = np.exp((n[:, None] + 1.0) * log_g[lane_head][None])
    k_dec = np.exp((ls - 1.0 - n)[:, None] * log_g[lane_head][None])
    c_dec = np.exp(ls * log_g[lane_head])[:, None] * np.ones((1, LANES))
    c_dec = np.tile(c_dec.reshape(N_PAIR, 1, LANES, LANES), (1, bb, 1, 1)).reshape(N_PAIR, bb * LANES, LANES)
    f32 = lambda a: jnp.asarray(a, dtype=F32)
    return f32(dmask), f32(q_dec), f32(k_dec), f32(c_dec)


def _rope_tables(t, pos0):
    half = DK // 2
    inv = ROPE_BASE ** (-np.arange(half, dtype=np.float64) / half)
    ang = (np.arange(t, dtype=np.float64) + pos0)[:, None] * inv[None]
    cos = np.tile(np.cos(ang), (1, 2 * LANES // DK))
    sin = np.tile(np.concatenate([-np.sin(ang), np.sin(ang)], axis=1), (1, LANES // DK))
    return jnp.asarray(cos, dtype=F32), jnp.asarray(sin, dtype=F32)


def _rope(x, cos, sin_signed):
    lane = lax.broadcasted_iota(jnp.int32, (1, LANES), 1) % DK
    swapped = jnp.where(lane < DK // 2, pltpu.roll(x, LANES - DK // 2, 1), pltpu.roll(x, DK // 2, 1))
    return x * cos + swapped * sin_signed


def _ret_body(q, k, v, g, cos, sin, dmask, q_dec, k_dec, c_dec, gn, s_stack, bb, ls):
    hm = _head_masks()
    qr = _rope(q, cos, sin)
    kr = _rope(k, cos, sin) * (DK ** -0.5)
    kr_bf = kr.astype(BF16)
    q_in = qr * q_dec
    k_end = kr * k_dec
    s_bf = s_stack.astype(BF16)
    outs = []
    upd = None
    for h in range(2):
        vh = v[:, h * DV:(h + 1) * DV].astype(BF16)
        att = _dot_nt(jnp.where(hm[h], qr, 0.0).astype(BF16), kr_bf) * dmask[h]
        o = jnp.dot(att.astype(BF16), vh, preferred_element_type=F32)
        o = o + jnp.dot(_expand(jnp.where(hm[h], q_in, 0.0), bb, ls).astype(BF16), s_bf,
                        preferred_element_type=F32)
        d = o - jnp.mean(o, axis=-1, keepdims=True)
        var = jnp.mean(d * d, axis=-1, keepdims=True)
        o = d * lax.rsqrt(var + EPS) * gn[:, h * DV:(h + 1) * DV]
        outs.append(o * _silu(g[:, h * DV:(h + 1) * DV]))
        u = _dot_tn(_expand(jnp.where(hm[h], k_end, 0.0), bb, ls).astype(BF16), vh)
        upd = u if upd is None else upd + u
    return jnp.concatenate(outs, axis=1), s_stack * c_dec + upd


def _ret_kernel(cfg, *refs):
    bb, ls, nchunk, has_s0 = cfg
    (q_ref, k_ref, v_ref, g_ref, cos_ref, sin_ref, dm_ref, qd_ref, kd_ref, cd_ref, gn_ref) = refs[:11]
    if has_s0:
        s0_ref, o_ref, s_ref = refs[11:]
    else:
        o_ref, s_ref = refs[11:]

    @pl.when(pl.program_id(2) == 0)
    def _():
        s_ref[...] = s0_ref[...] if has_s0 else jnp.zeros(s_ref.shape, F32)

    rows = bb * ls
    s = s_ref[...].reshape(bb * LANES, LANES)
    for ci in range(nchunk):
        sl = slice(ci * rows, (ci + 1) * rows)
        o, s = _ret_body(q_ref[sl, :], k_ref[sl, :], v_ref[sl, :], g_ref[sl, :], cos_ref[sl, :],
                         sin_ref[sl, :], dm_ref[...], qd_ref[...], kd_ref[...], cd_ref[...],
                         gn_ref[...], s, bb, ls)
        o_ref[sl, :] = o.astype(o_ref.dtype)
    s_ref[...] = s.reshape(s_ref.shape)


def _ret_call(p, gn, s0, *, n_seq, t, bb, ls, nchunk, pos0):
    rows_total = n_seq * t
    tl, grid, row = _mixer_grid(n_seq, t, bb, ls, nchunk)
    has_s0 = s0 is not None
    dmask, q_dec, k_dec, c_dec = _ret_tables(bb, ls, pos0)
    cos, sin = _rope_tables(t, pos0)
    r = bb * ls
    if bb > 1:
        cos, sin = jnp.tile(cos, (bb, 1)), jnp.tile(sin, (bb, 1))
        rope_spec = pl.BlockSpec((tl, LANES), lambda o, j, c: (0, 0))
    else:
        rope_spec = pl.BlockSpec((tl, LANES), lambda o, j, c: (c, 0))

    def col(width, base):
        return pl.BlockSpec((tl, width), lambda o, j, c: (row(o, j, c), base // width + j))

    st_spec = pl.BlockSpec((bb, None, LANES, LANES), lambda o, j, c: (o, j, 0, 0))
    in_specs = [col(LANES, COL_QC), col(LANES, COL_KC), col(2 * DV, COL_VC), col(2 * DV, COL_GC),
                rope_spec, rope_spec,
                pl.BlockSpec((2, r, r), lambda o, j, c: (j, 0, 0)),
                pl.BlockSpec((r, LANES), lambda o, j, c: (0, j)),
                pl.BlockSpec((r, LANES), lambda o, j, c: (0, j)),
                pl.BlockSpec((None, bb * LANES, LANES), lambda o, j, c: (j, 0, 0)),
                pl.BlockSpec((1, 2 * DV), lambda o, j, c: (0, j))]
    args = [p, p, p, p, cos, sin, dmask, q_dec, k_dec, c_dec, gn]
    if has_s0:
        in_specs.append(st_spec)
        args.append(s0)
    return pl.pallas_call(
        functools.partial(_ret_kernel, (bb, ls, nchunk, has_s0)),
        grid=grid, in_specs=in_specs,
        out_specs=[pl.BlockSpec((tl, 2 * DV), lambda o, j, c: (row(o, j, c), j)), st_spec],
        out_shape=[jax.ShapeDtypeStruct((rows_total, N_HEADS * DV), BF16),
                   jax.ShapeDtypeStruct((n_seq, N_PAIR, LANES, LANES), F32)],
        compiler_params=_cparams(("arbitrary", "arbitrary", "arbitrary")),
        name="ret_mixer",
    )(*args)


def _gelu(x):
    return 0.5 * x * (1.0 + jnp.tanh(0.7978845608028654 * (x + 0.044715 * x * x * x)))


def _cmlp_kernel(n_tiles, want_v, u_ref, v_ref, w_ref, bias_ref, lg_ref, lb_ref, *out_refs):
    o_ref = out_refs[0]
    for ti in range(n_tiles):
        sl = slice(ti * CHUNK_B, (ti + 1) * CHUNK_B)
        u = _gelu(u_ref[sl, :])
        vv = _gelu(v_ref[sl, :])
        vn_parts, z_parts = [], []
        for gi in range(G_B):
            vg = vv[:, gi * CG_B:(gi + 1) * CG_B]
            d = vg - jnp.mean(vg, axis=-1, keepdims=True)
            vn = d * lax.rsqrt(jnp.mean(d * d, axis=-1, keepdims=True) + EPS)
            vn = vn * lg_ref[:, gi * CG_B:(gi + 1) * CG_B] + lb_ref[:, gi * CG_B:(gi + 1) * CG_B]
            vn_parts.append(vn)
            z_parts.append(jnp.dot(w_ref[gi], vn.astype(BF16), preferred_element_type=F32))
        z = jnp.concatenate(z_parts, axis=1) + bias_ref[...]
        o_ref[sl, :] = (u * z).astype(o_ref.dtype)
        if want_v:
            out_refs[1][sl, :] = jnp.concatenate(vn_parts, axis=1)


def _cmlp_call(p, w_mix, bias_tile, ln_g, ln_b, *, n_tiles, want_v):
    rows = p.shape[0]
    tl = n_tiles * CHUNK_B
    const = lambda shape: pl.BlockSpec(shape, lambda i: (0,) * len(shape))
    out_specs = [pl.BlockSpec((tl, W_B), lambda i: (i, 0))]
    out_shape = [jax.ShapeDtypeStruct((rows, W_B), BF16)]
    if want_v:
        out_specs.append(pl.BlockSpec((tl, W_B), lambda i: (i, 0)))
        out_shape.append(jax.ShapeDtypeStruct((rows, W_B), F32))
    out = pl.pallas_call(
        functools.partial(_cmlp_kernel, n_tiles, want_v),
        grid=(rows // tl,),
        in_specs=[pl.BlockSpec((tl, W_B), lambda i: (i, COL_UB // W_B)),
                  pl.BlockSpec((tl, W_B), lambda i: (i, COL_VB // W_B)),
                  const((G_B, CHUNK_B, CHUNK_B)), const((CHUNK_B, W_B)), const((1, W_B)), const((1, W_B))],
        out_specs=out_specs, out_shape=out_shape,
        compiler_params=_cparams(("arbitrary",)),
        name="cmlp_mixer",
    )(p, p, w_mix, bias_tile, ln_g, ln_b)
    return (out[0], out[1]) if want_v else (out[0], None)


_HI16 = 0xFFFF0000


def _pack_bf16_pairs(x):
    n = x.shape[1] // 2
    lo = lax.bitcast_convert_type(x[:, :n].astype(BF16).astype(F32), jnp.uint32) >> 16
    hi = lax.bitcast_convert_type(x[:, n:].astype(BF16).astype(F32), jnp.uint32) & jnp.uint32(_HI16)
    return hi | lo


def _unpack_bf16_pairs(u):
    lo = lax.bitcast_convert_type(u << 16, F32).astype(BF16)
    hi = lax.bitcast_convert_type(u & jnp.uint32(_HI16), F32).astype(BF16)
    return jnp.concatenate([lo, hi], axis=1)


def _outproj_kernel(oa_ref, ob_ref, oc_ref, w_ref, x_ref, g1_ref, gn_ref, sc_ref, sh_ref,
                    wr_ref, br_ref, xo_ref, h_ref, lg_ref):
    wa = N_HEADS * DV
    acc = jnp.dot(oa_ref[...], w_ref[0:wa, :], preferred_element_type=F32)
    acc = acc + jnp.dot(ob_ref[...], w_ref[wa:wa + W_B, :], preferred_element_type=F32)
    acc = acc + jnp.dot(oc_ref[...], w_ref[wa + W_B:, :], preferred_element_type=F32)
    x = x_ref[...] + g1_ref[...] * acc
    xo_ref[...] = x
    h = _rms_mod(x, gn_ref[...], sc_ref[...], sh_ref[...])
    h_ref[...] = _pack_bf16_pairs(h)
    lg_ref[...] = jnp.dot(h, wr_ref[...], preferred_element_type=F32, precision=HIGHEST) + br_ref[...]


def _outproj_call(oa, ob, oc, w_out, x, gate1, gnorm, scale, shift, w_router, b_router, *, tm,
                  tiles_per_group):
    rows, d = x.shape
    rspec = lambda width: pl.BlockSpec((tm, width), lambda i: (i, 0))
    const = lambda shape: pl.BlockSpec(shape, lambda i: (0,) * len(shape))
    ms = lambda m: _mod_spec(m, tiles_per_group)
    return pl.pallas_call(
        _outproj_kernel,
        grid=(rows // tm,),
        in_specs=[rspec(oa.shape[1]), rspec(ob.shape[1]), rspec(oc.shape[1]), const(w_out.shape),
                  rspec(d), ms(gate1), const((1, d)), ms(scale), ms(shift),
                  const(w_router.shape), const(b_router.shape)],
        out_specs=[rspec(d), rspec(d // 2), rspec(LANES)],
        out_shape=[jax.ShapeDtypeStruct((rows, d), F32), jax.ShapeDtypeStruct((rows, d // 2), jnp.uint32),
                   jax.ShapeDtypeStruct((rows, LANES), F32)],
        compiler_params=_cparams(("arbitrary",)),
        name="outproj_norm_router",
    )(oa, ob, oc, w_out, x, gate1, gnorm, scale, shift, w_router, b_router)


def _moe_in_kernel(se_ref, nb_ref, sx_ref, x_ref, wg_ref, wu_ref, bg_ref, bu_ref, h_ref, x_s, wg_s, wu_s):
    nb = nb_ref[pl.program_id(0)]

    @pl.when((nb > 0) & (pl.program_id(1) == 0))
    def _():
        for rb in range(MOE_SLAB // MOE_SUB):
            sl = slice(rb * MOE_SUB, (rb + 1) * MOE_SUB)

            @pl.when(rb < nb)
            def _():
                x_s[sl, :] = _unpack_bf16_pairs(x_ref[sl, :])

    @pl.when(nb > 0)
    def _():
        wg_s[...] = wg_ref[...].astype(BF16)
        wu_s[...] = wu_ref[...].astype(BF16)
        for rb in range(MOE_SLAB // MOE_SUB):
            sl = slice(rb * MOE_SUB, (rb + 1) * MOE_SUB)

            @pl.when(rb < nb)
            def _():
                xb = x_s[sl, :]
                gt = jnp.dot(xb, wg_s[...], preferred_element_type=F32) + bg_ref[...]
                up = jnp.dot(xb, wu_s[...], preferred_element_type=F32) + bu_ref[...]
                gt = jnp.minimum(gt, SWIGLU_LIMIT)
                up = jnp.clip(up, -SWIGLU_LIMIT, SWIGLU_LIMIT)
                glu = gt * jax.nn.sigmoid(SWIGLU_ALPHA * gt)
                h_ref[sl, :] = ((up + 1.0) * glu).astype(h_ref.dtype)

            @pl.when(rb >= nb)
            def _():
                h_ref[sl, :] = jnp.zeros((MOE_SUB, h_ref.shape[1]), h_ref.dtype)


def _moe_out_kernel(se_ref, nb_ref, sx_ref, h_ref, w_ref, b_ref, y_ref, w_s):
    nb = nb_ref[pl.program_id(0)]

    @pl.when(nb > 0)
    def _():
        w_s[...] = w_ref[...].astype(BF16)
        for rb in range(MOE_SLAB // MOE_SUB):
            sl = slice(rb * MOE_SUB, (rb + 1) * MOE_SUB)

            @pl.when(rb < nb)
            def _():
                y_ref[sl, :] = jnp.dot(h_ref[sl, :], w_s[...], preferred_element_type=F32) + b_ref[...]

            @pl.when(rb >= nb)
            def _():
                y_ref[sl, :] = jnp.zeros((MOE_SUB, y_ref.shape[1]), y_ref.dtype)


def _moe_calls(x_slabs, slab_e, slab_nb, slab_x, w_in, b_in, w_out, b_out, layer):
    n_slab, r, dh = x_slabs.shape
    d = 2 * dh
    nt1 = D_FF // MOE_TN1
    nt2 = d // MOE_TN2
    up_off = D_FF // MOE_TN1

    def jj(j, s, nb, last):
        return jnp.where(nb[s] > 0, j, last)

    def wspec(k, tn, last, off=0):
        return pl.BlockSpec((None, None, k, tn),
                            lambda s, j, se, nb, sx: (layer, se[s], 0, off + jj(j, s, nb, last)))

    h = pl.pallas_call(
        _moe_in_kernel,
        grid_spec=pltpu.PrefetchScalarGridSpec(
            num_scalar_prefetch=3, grid=(n_slab, nt1),
            in_specs=[
                pl.BlockSpec((None, r, dh), lambda s, j, se, nb, sx: (sx[s], 0, 0)),
                wspec(d, MOE_TN1, nt1 - 1), wspec(d, MOE_TN1, nt1 - 1, up_off),
                wspec(1, MOE_TN1, nt1 - 1), wspec(1, MOE_TN1, nt1 - 1, up_off),
            ],
            out_specs=pl.BlockSpec((None, r, MOE_TN1), lambda s, j, se, nb, sx: (sx[s], 0, jj(j, s, nb, nt1 - 1))),
            scratch_shapes=[pltpu.VMEM((r, d), BF16), pltpu.VMEM((d, MOE_TN1), BF16),
                            pltpu.VMEM((d, MOE_TN1), BF16)]),
        out_shape=jax.ShapeDtypeStruct((n_slab, r, D_FF), BF16),
        compiler_params=_cparams(("arbitrary", "arbitrary")),
        name="moe_glu",
    )(slab_e, slab_nb, slab_x, x_slabs, w_in, w_in, b_in, b_in)

    y = pl.pallas_call(
        _moe_out_kernel,
        grid_spec=pltpu.PrefetchScalarGridSpec(
            num_scalar_prefetch=3, grid=(n_slab, nt2),
            in_specs=[
                pl.BlockSpec((None, r, D_FF), lambda s, j, se, nb, sx: (sx[s], 0, 0)),
                wspec(D_FF, MOE_TN2, nt2 - 1), wspec(1, MOE_TN2, nt2 - 1),
            ],
            out_specs=pl.BlockSpec((None, r, MOE_TN2), lambda s, j, se, nb, sx: (sx[s], 0, jj(j, s, nb, nt2 - 1))),
            scratch_shapes=[pltpu.VMEM((D_FF, MOE_TN2), BF16)]),
        out_shape=jax.ShapeDtypeStruct((n_slab, r, d), F32),
        compiler_params=_cparams(("arbitrary", "arbitrary")),
        name="moe_down",
    )(slab_e, slab_nb, slab_x, h, w_out, b_out)
    return y


def _combine_kernel(y_ref, g_ref, o_ref):
    acc = g_ref[:, 0:1] * y_ref[0]
    for k in range(1, TOP_K):
        acc = acc + g_ref[:, k:k + 1] * y_ref[k]
    o_ref[...] = acc


def _combine_call(y4, gates, tm=256):
    k, m, d = y4.shape
    return pl.pallas_call(
        _combine_kernel, grid=(m // tm,),
        in_specs=[pl.BlockSpec((k, tm, d), lambda i: (0, i, 0)), pl.BlockSpec((tm, k), lambda i: (i, 0))],
        out_specs=pl.BlockSpec((tm, d), lambda i: (i, 0)),
        out_shape=jax.ShapeDtypeStruct((m, d), F32),
        compiler_params=_cparams(("arbitrary",)),
        name="moe_combine",
    )(y4, gates)


def _rows(a, idx):
    return a.at[idx].get(mode="promise_in_bounds")


def _n_slabs(n_assign):
    return N_EXPERTS + (n_assign + N_EXPERTS * (MOE_SUB - 1)) // MOE_SLAB


def _route(logits, n_slab):
    m = logits.shape[0]
    top_val, top_idx = lax.top_k(logits, TOP_K)
    gates = jax.nn.softmax(top_val, axis=-1)
    a = m * TOP_K
    flat_e = top_idx.reshape(a).astype(jnp.int32)
    ar = jnp.arange(a, dtype=jnp.int32)
    sorted_e, order = lax.sort((flat_e, ar), num_keys=1, is_stable=True)
    counts = jnp.bincount(flat_e, length=N_EXPERTS).astype(jnp.int32)
    start = jnp.cumsum(counts) - counts
    rank = ar - start[sorted_e]
    padded = (counts + MOE_SUB - 1) // MOE_SUB * MOE_SUB
    nslab_e = (padded + MOE_SLAB - 1) // MOE_SLAB
    slab_end = jnp.cumsum(nslab_e)
    slab_start = slab_end - nslab_e
    dest = (slab_start[sorted_e] + rank // MOE_SLAB) * MOE_SLAB + rank % MOE_SLAB
    _, pos = lax.sort((order, dest), num_keys=1)
    pos = pos.reshape(m, TOP_K)
    n_used = slab_end[-1]
    sid = jnp.arange(n_slab, dtype=jnp.int32)
    used = sid < n_used
    sid_eff = jnp.minimum(sid, n_used - 1)
    slab_e = jnp.minimum(jnp.searchsorted(slab_end, sid_eff, side="right"), N_EXPERTS - 1).astype(jnp.int32)
    local = sid_eff - slab_start[slab_e]
    valid = jnp.clip(padded[slab_e] - local * MOE_SLAB, 0, MOE_SLAB)
    slab_nb = jnp.where(used, valid // MOE_SUB, 0).astype(jnp.int32)
    r_idx = jnp.arange(MOE_SLAB, dtype=jnp.int32)[None, :]
    rk = local[:, None] * MOE_SLAB + r_idx
    has_tok = used[:, None] & (rk < counts[slab_e][:, None])
    src = jnp.clip(start[slab_e][:, None] + rk, 0, a - 1)
    row_tok = jnp.where(has_tok, _rows(order, src) // TOP_K, (sid[:, None] * MOE_SLAB + r_idx) % m)
    return gates, pos, row_tok.reshape(-1), slab_e, slab_nb, sid_eff.astype(jnp.int32)


def _moe(h2u, logits, w_in, b_in, w_out, b_out, layer):
    m, dh = h2u.shape
    d = 2 * dh
    n_slab = _n_slabs(m * TOP_K)
    gates, pos, row_tok, slab_e, slab_nb, slab_x = _route(logits, n_slab)
    x_slabs = _rows(h2u, row_tok).reshape(n_slab, MOE_SLAB, dh)
    y = _moe_calls(x_slabs, slab_e, slab_nb, slab_x, w_in, b_in.reshape(b_in.shape[0], N_EXPERTS, 1, -1),
                   w_out, b_out.reshape(b_out.shape[0], N_EXPERTS, 1, -1), layer)
    y4 = _rows(y.reshape(n_slab * MOE_SLAB, d), pos.T.reshape(-1))
    return _combine_call(y4.reshape(TOP_K, m, d), gates)


def _final_kernel(x_ref, y_ref, g2_ref, gn_ref, o_ref):
    x = x_ref[...] + g2_ref[...] * y_ref[...]
    ms = jnp.mean(x * x, axis=-1, keepdims=True)
    o_ref[...] = x * lax.rsqrt(ms + EPS) * gn_ref[...]


def _final_call(x, y2, gate2, gnorm, *, tm, tiles_per_group):
    rows, d = x.shape
    rspec = pl.BlockSpec((tm, d), lambda i: (i, 0))
    return pl.pallas_call(
        _final_kernel, grid=(rows // tm,),
        in_specs=[rspec, rspec, _mod_spec(gate2, tiles_per_group), pl.BlockSpec((1, d), lambda i: (0, 0))],
        out_specs=rspec, out_shape=jax.ShapeDtypeStruct((rows, d), F32),
        compiler_params=_cparams(("arbitrary",)),
        name="final_norm",
    )(x, y2, gate2, gnorm)


def _prep_w_in(w_in):
    depth, d, _ = w_in.shape
    pad = jnp.zeros((depth, d, D_P - COL_RA - GLA_LOWRANK), w_in.dtype)
    w = jnp.concatenate([w_in[:, :, _O_UB:_O_QC], w_in[:, :, :_O_RA], w_in[:, :, _O_QC:_O_END],
                         w_in[:, :, _O_RA:_O_UB], pad], axis=-1)
    return w.astype(BF16)


def _prep_cmlp(w_spatial, b_spatial, t):
    tri = jnp.tril(jnp.ones((CHUNK_B, CHUNK_B), bool))
    w = jnp.where(tri[None, None], w_spatial, 0.0)
    if t < CHUNK_B:
        reps = CHUNK_B // t
        eye = jnp.eye(reps, dtype=w.dtype)
        w = jnp.einsum("ab,lgij->lgaibj", eye, w[:, :, :t, :t]).reshape(w.shape[0], G_B, CHUNK_B, CHUNK_B)
        b = jnp.tile(b_spatial[:, :, :t], (1, 1, reps))
    else:
        b = b_spatial
    bias = jnp.repeat(jnp.swapaxes(b, 1, 2), CG_B, axis=2)
    return w.astype(BF16), bias.astype(F32)


def kernel(x_prompt, x_sample, c_prompt, c_sample, state_gla, state_ret, w_ada, b_ada, g_norm1,
           g_norm2, w_in, w_gla_gate, b_gla_gate, g_gla_out, ln_g_cmlp, ln_b_cmlp, w_spatial,
           b_spatial, g_ret_out, w_out, w_router, b_router, w_e_in, b_e_in, w_e_out, b_e_out, g_final):
    bp, tp, d = x_prompt.shape
    bs, ts, _ = x_sample.shape
    rows_p, rows_s = bp * tp, bs * ts
    tm_p, tm_s = 512, 256
    tmo_p, tmo_s = 512, 256
    tm_f = 256
    dec_bb = LANES // ts

    n_c = bp + bs
    n_c_pad = -(-n_c // SUBLANES) * SUBLANES
    c_all = jnp.concatenate([c_prompt, c_sample, jnp.zeros((n_c_pad - n_c, d), F32)], axis=0)
    mod = _ada_call(c_all, w_ada, b_ada).reshape(DEPTH, n_c_pad, N_MOD, d)
    mod_p = jnp.transpose(mod[:, :bp], (0, 2, 1, 3)).reshape(DEPTH, N_MOD, bp, 1, d)
    mod_s = jnp.repeat(jnp.transpose(mod[:, bp:n_c], (0, 2, 1, 3)), ts, axis=2)

    w_in_r = _prep_w_in(w_in)
    w_out_bf = w_out.astype(BF16)
    wg_pad = jnp.pad(w_gla_gate, ((0, 0), (0, LANES - GLA_LOWRANK), (0, 0))).astype(BF16)
    w_router_pad = jnp.pad(w_router, ((0, 0), (0, 0), (0, LANES - N_EXPERTS)))
    b_router_pad = jnp.pad(b_router, ((0, 0), (0, LANES - N_EXPERTS))).reshape(DEPTH, 1, LANES)
    wmix_p, bias_p = _prep_cmlp(w_spatial, b_spatial, CHUNK_B)
    wmix_s, bias_s = _prep_cmlp(w_spatial, b_spatial, ts)

    xp = x_prompt.reshape(rows_p, d)
    xs = x_sample.reshape(rows_s, d)
    st_gla = state_gla.reshape(DEPTH, bs, N_PAIR, LANES, LANES)
    st_ret = state_ret.reshape(DEPTH, bs, N_PAIR, LANES, LANES)
    y2p = y2s = None
    outs = {k: [] for k in ("gla_p", "ret_p", "gla_s", "ret_s", "v_s")}

    for l in range(DEPTH):
        row = lambda a: a[l].reshape(1, -1)
        mp = lambda i: mod_p[l, i]
        msf = lambda i, tm: mod_s[l, i].reshape(rows_s // tm, tm, d)
        g2p = mod_p[l - 1, 5] if l > 0 else None
        g2s = (lambda tm: mod_s[l - 1, 5].reshape(rows_s // tm, tm, d)) if l > 0 else (lambda tm: None)

        xp, pp = _inproj_call(xp, y2p, g2p, row(g_norm1), mp(1), mp(0), w_in_r[l],
                              tm=tm_p, tn=640, tiles_per_group=tp // tm_p)
        xs, ps = _inproj_call(xs, y2s, g2s(tm_s), row(g_norm1), msf(1, tm_s), msf(0, tm_s), w_in_r[l],
                              tm=tm_s, tn=640, tiles_per_group=1)

        bg = row(b_gla_gate)
        gout = row(g_gla_out)
        gret = row(g_ret_out)
        oa_p, sa_p = _gla_call(pp, wg_pad[l], bg, gout, None, n_seq=bp, t=tp, bb=1, ls=GLA_CHUNK, nchunk=4)
        oa_s, sa_s = _gla_call(ps, wg_pad[l], bg, gout, st_gla[l], n_seq=bs, t=ts, bb=dec_bb, ls=ts, nchunk=1)
        oc_p, sc_p = _ret_call(pp, gret, None, n_seq=bp, t=tp, bb=1, ls=RET_CHUNK, nchunk=2, pos0=0)
        oc_s, sc_s = _ret_call(ps, gret, st_ret[l], n_seq=bs, t=ts, bb=dec_bb, ls=ts, nchunk=1, pos0=PAST_LEN)
        ob_p, _ = _cmlp_call(pp, wmix_p[l], bias_p[l], row(ln_g_cmlp), row(ln_b_cmlp), n_tiles=4, want_v=False)
        ob_s, v_s = _cmlp_call(ps, wmix_s[l], bias_s[l], row(ln_g_cmlp), row(ln_b_cmlp), n_tiles=4, want_v=True)

        xp, h2p, lgp = _outproj_call(oa_p, ob_p, oc_p, w_out_bf[l], xp, mp(2), row(g_norm2), mp(4), mp(3),
                                     w_router_pad[l], b_router_pad[l], tm=tmo_p, tiles_per_group=tp // tmo_p)
        xs, h2s, lgs = _outproj_call(oa_s, ob_s, oc_s, w_out_bf[l], xs, msf(2, tmo_s), row(g_norm2),
                                     msf(4, tmo_s), msf(3, tmo_s), w_router_pad[l], b_router_pad[l],
                                     tm=tmo_s, tiles_per_group=1)

        h2 = jnp.concatenate([h2p, h2s], axis=0)
        logits = jnp.concatenate([lgp, lgs], axis=0)[:, :N_EXPERTS]
        y2 = _moe(h2, logits, w_e_in, b_e_in, w_e_out, b_e_out, l)
        y2p, y2s = y2[:rows_p], y2[rows_p:]

        outs["gla_p"].append(sa_p.reshape(bp, N_HEADS, DK, DV))
        outs["ret_p"].append(sc_p.reshape(bp, N_HEADS, DK, DV))
        outs["gla_s"].append(sa_s.reshape(bs, N_HEADS, DK, DV))
        outs["ret_s"].append(sc_s.reshape(bs, N_HEADS, DK, DV))
        outs["v_s"].append(v_s.reshape(bs, ts, W_B))

    yp = _final_call(xp, y2p, mod_p[DEPTH - 1, 5], g_final.reshape(1, d), tm=tm_f, tiles_per_group=tp // tm_f)
    ys = _final_call(xs, y2s, mod_s[DEPTH - 1, 5].reshape(rows_s // tm_f, tm_f, d), g_final.reshape(1, d),
                     tm=tm_f, tiles_per_group=1)
    return (yp.reshape(bp, tp, d), ys.reshape(bs, ts, d), jnp.stack(outs["gla_p"]), jnp.stack(outs["ret_p"]),
            jnp.stack(outs["gla_s"]), jnp.stack(outs["ret_s"]), jnp.stack(outs["v_s"]))
```

```python
import functools

import numpy as np
import jax
import jax.numpy as jnp
from jax import lax
from jax.experimental import pallas as pl
from jax.experimental.pallas import tpu as pltpu

F32 = jnp.float32
BF16 = jnp.bfloat16
HIGHEST = lax.Precision.HIGHEST

D_MODEL = 2048
DEPTH = 4
N_HEADS = 6
N_PAIR = N_HEADS // 2
DK = 64
DV = 128
GLA_LOWRANK = 16
GLA_TAU = 16.0
GLA_CHUNK = 64
GLA_SUB = 16
G_B = 4
CG_B = 128
W_B = G_B * CG_B
CHUNK_B = 128
RET_CHUNK = 128
ROPE_BASE = 10000.0
N_EXPERTS = 32
TOP_K = 4
D_FF = D_MODEL
SWIGLU_LIMIT = 7.0
SWIGLU_ALPHA = 1.702
N_MOD = 6
EPS = 1e-6
PAST_LEN = 16384

LANES = 128
SUBLANES = 8
VMEM_LIMIT = 56 * 1024 * 1024

COL_UB = 0
COL_VB = 512
COL_QA = 1024
COL_KA = 1408
COL_VA = 1792
COL_GA = 2560
COL_QC = 3328
COL_KC = 3712
COL_VC = 4096
COL_GC = 4864
COL_RA = 5632
D_P = 5760
_O_RA = 2304
_O_UB = 2320
_O_QC = 3344
_O_END = 5648

EXP_CLAMP = 80.0

MOE_SUB = 256
MOE_SLAB = 1536
MOE_TN1 = 512
MOE_TN2 = 512


def _cparams(sem):
    return pltpu.CompilerParams(dimension_semantics=sem, vmem_limit_bytes=VMEM_LIMIT)


def _ada_kernel(c_ref, w_ref, b_ref, o_ref):
    c = c_ref[...]
    s = (c * jax.nn.sigmoid(c)).astype(BF16)
    o_ref[...] = jnp.dot(s, w_ref[...].astype(BF16), preferred_element_type=F32) + b_ref[...]


def _ada_call(c_all, w_ada, b_ada, tn=1024):
    nb, d = c_all.shape
    depth, _, n = w_ada.shape
    return pl.pallas_call(
        _ada_kernel,
        grid=(depth, n // tn),
        in_specs=[
            pl.BlockSpec((nb, d), lambda l, j: (0, 0)),
            pl.BlockSpec((None, d, tn), lambda l, j: (l, 0, j)),
            pl.BlockSpec((None, 1, tn), lambda l, j: (l, 0, j)),
        ],
        out_specs=pl.BlockSpec((None, nb, tn), lambda l, j: (l, 0, j)),
        out_shape=jax.ShapeDtypeStruct((depth, nb, n), F32),
        compiler_params=_cparams(("arbitrary", "arbitrary")),
        name="ada_mod",
    )(c_all, w_ada, b_ada.reshape(depth, 1, n))


def _rms_mod(x, gn, sc, sh):
    ms = jnp.mean(x * x, axis=-1, keepdims=True)
    h = x * lax.rsqrt(ms + EPS) * gn
    return h * (1.0 + sc) + sh


_HI16 = 0xFFFF0000


def _pack_bf16_pairs(x, group):
    outs = []
    for c0 in range(0, x.shape[1], 2 * group):
        lo = lax.bitcast_convert_type(x[:, c0:c0 + group].astype(BF16).astype(F32), jnp.uint32) >> 16
        hi = lax.bitcast_convert_type(x[:, c0 + group:c0 + 2 * group].astype(BF16).astype(F32), jnp.uint32)
        outs.append((hi & jnp.uint32(_HI16)) | lo)
    return outs[0] if len(outs) == 1 else jnp.concatenate(outs, axis=1)


def _unpack_bf16_pairs(u, group):
    lo = lax.bitcast_convert_type(u << 16, F32)
    hi = lax.bitcast_convert_type(u & jnp.uint32(_HI16), F32)
    parts = []
    for c0 in range(0, u.shape[1], group):
        parts += [lo[:, c0:c0 + group], hi[:, c0:c0 + group]]
    return jnp.concatenate(parts, axis=1)


def _resid_norm_kernel(has_y, final, *refs):
    it = iter(refs)
    x_ref = next(it)
    if has_y:
        y_ref, gt_ref, g2_ref = next(it), next(it), next(it)
    gn_ref = next(it)
    if not final:
        sc_ref, sh_ref = next(it), next(it)
    if has_y and not final:
        xo_ref = next(it)
    o_ref = next(it)

    x = x_ref[...]
    if has_y:
        acc = None
        for k in range(TOP_K):
            t = gt_ref[:, k:k + 1] * _unpack_bf16_pairs(y_ref[k], MOE_TN2 // 2)
            acc = t if acc is None else acc + t
        x = x + g2_ref[...] * acc
        if not final:
            xo_ref[...] = x
    if final:
        ms = jnp.mean(x * x, axis=-1, keepdims=True)
        o_ref[...] = x * lax.rsqrt(ms + EPS) * gn_ref[...]
    else:
        o_ref[...] = _rms_mod(x, gn_ref[...], sc_ref[...], sh_ref[...]).astype(o_ref.dtype)


def _mod_spec(mod, tiles_per_group):
    _, r, d = mod.shape
    return pl.BlockSpec((None, r, d), lambda i, j=0: (i // tiles_per_group, 0, 0))


def _resid_norm_call(x, y4, gates, gate2, gnorm, scale, shift, *, tm, tiles_per_group, row_off, final):
    rows, d = x.shape
    has_y = y4 is not None
    row_spec = pl.BlockSpec((tm, d), lambda i: (i, 0))
    in_specs, args = [row_spec], [x]
    if has_y:
        off = row_off // tm
        in_specs += [pl.BlockSpec((TOP_K, tm, d // 2), lambda i: (0, i + off, 0)),
                     pl.BlockSpec((tm, TOP_K), lambda i: (i + off, 0)), _mod_spec(gate2, tiles_per_group)]
        args += [y4, gates, gate2]
    in_specs.append(pl.BlockSpec((1, d), lambda i: (0, 0)))
    args.append(gnorm)
    if not final:
        in_specs += [_mod_spec(scale, tiles_per_group), _mod_spec(shift, tiles_per_group)]
        args += [scale, shift]
    if final:
        out_specs, out_shape = row_spec, jax.ShapeDtypeStruct((rows, d), F32)
    elif has_y:
        out_specs = [row_spec, row_spec]
        out_shape = [jax.ShapeDtypeStruct((rows, d), F32), jax.ShapeDtypeStruct((rows, d), BF16)]
    else:
        out_specs, out_shape = row_spec, jax.ShapeDtypeStruct((rows, d), BF16)
    out = pl.pallas_call(
        functools.partial(_resid_norm_kernel, has_y, final),
        grid=(rows // tm,), in_specs=in_specs, out_specs=out_specs, out_shape=out_shape,
        compiler_params=_cparams(("arbitrary",)),
        name="resid_norm",
    )(*args)
    if final:
        return out
    return (out[0], out[1]) if has_y else (x, out)


def _matmul_kernel(x_ref, w_ref, o_ref):
    o_ref[...] = jnp.dot(x_ref[...], w_ref[...], preferred_element_type=F32)


def _inproj_call(h, w, *, tm, tn):
    rows, d = h.shape
    n = w.shape[1]
    return pl.pallas_call(
        _matmul_kernel,
        grid=(rows // tm, n // tn),
        in_specs=[pl.BlockSpec((tm, d), lambda i, j: (i, 0)), pl.BlockSpec((d, tn), lambda i, j: (0, j))],
        out_specs=pl.BlockSpec((tm, tn), lambda i, j: (i, j)),
        out_shape=jax.ShapeDtypeStruct((rows, n), F32),
        compiler_params=_cparams(("arbitrary", "arbitrary")),
        name="inproj",
    )(h, w)


def _head_masks(width=LANES):
    lane = lax.broadcasted_iota(jnp.int32, (1, width), 1) % LANES
    return lane < DK, lane >= DK


def _expand(x, bb, ls):
    if bb == 1:
        return x
    seq = lax.broadcasted_iota(jnp.int32, (bb * ls, 1), 0) // ls
    return jnp.concatenate([jnp.where(seq == b, x, 0.0) for b in range(bb)], axis=1)


def _dot_nt(a, b):
    return lax.dot_general(a, b, (((1,), (1,)), ((), ())), preferred_element_type=F32)


def _dot_tn(a, b, precision=None):
    return lax.dot_general(a, b, (((0,), (0,)), ((), ())), preferred_element_type=F32,
                           precision=precision)


def _silu(x):
    return x * jax.nn.sigmoid(x)


def _seq_consts(bb, ls, sub):
    rows = bb * ls
    t = np.arange(rows)
    same_seq = (t[:, None] // ls) == (t[None, :] // ls)
    causal = same_seq & (t[None, :] <= t[:, None])
    sub_start = (t // sub) * sub
    in_sub = causal & (t[None, :] >= sub_start[:, None])
    return in_sub.astype(np.float32), same_seq.astype(np.float32), causal.astype(np.float32)


def _gla_body(q, k, v, g, ra, wg, bg, gout, mt, mo, cmask, s_stack, bb, ls):
    rows = bb * ls
    sub = min(GLA_SUB, ls)
    n_sub = ls // sub
    assert bb == 1 or n_sub == 1
    hm = _head_masks()

    x = jnp.dot(ra.astype(BF16), wg, preferred_element_type=F32) + bg
    log_a = (jnp.minimum(x, 0.0) - jnp.log1p(jnp.exp(-jnp.abs(x)))) * (1.0 / GLA_TAU)
    bs = jnp.dot(mt, log_a, preferred_element_type=F32, precision=HIGHEST)
    if n_sub > 1:
        bcum = jnp.dot(cmask, log_a, preferred_element_type=F32, precision=HIGHEST)
    else:
        bcum = bs
    bend = jnp.dot(mo, log_a, preferred_element_type=F32, precision=HIGHEST)

    qs = q * (DK ** -0.5)
    q_sub = qs * jnp.exp(bs)
    q_seq = qs * jnp.exp(bcum)
    k_end = k * jnp.exp(bend - bcum)

    sub_id = (lax.broadcasted_iota(jnp.int32, (rows, 1), 0) % ls) // sub
    a_parts, k_parts = [], []
    for i in range(n_sub):
        if i == 0:
            k_i = k * jnp.exp(jnp.minimum(-bcum, EXP_CLAMP))
        else:
            k_i = k * jnp.exp(jnp.minimum(bcum[i * sub - 1:i * sub, :] - bcum, EXP_CLAMP))
        k_parts.append(k_i.astype(BF16))
        a_parts.append(jnp.where(sub_id == i, q_sub, 0.0) if n_sub > 1 else q_sub)
    a = a_parts[0] if n_sub == 1 else jnp.concatenate(a_parts, axis=1)
    kk = k_parts[0] if n_sub == 1 else jnp.concatenate(k_parts, axis=1)
    hmw = _head_masks(n_sub * LANES)
    a2 = jnp.concatenate([jnp.where(hmw[0], a, 0.0), jnp.where(hmw[1], a, 0.0)], axis=0).astype(BF16)
    att = _dot_nt(a2, kk)
    att = jnp.where(jnp.concatenate([cmask, cmask], axis=0) > 0.0, att, 0.0)

    s_bf = s_stack.astype(BF16)
    outs = []
    upd = None
    for h in range(2):
        vh = v[:, h * DV:(h + 1) * DV].astype(BF16)
        o = jnp.dot(att[h * rows:(h + 1) * rows].astype(BF16), vh, preferred_element_type=F32)
        o = o + jnp.dot(_expand(jnp.where(hm[h], q_seq, 0.0), bb, ls).astype(BF16), s_bf,
                        preferred_element_type=F32)
        o = o * lax.rsqrt(jnp.mean(o * o, axis=-1, keepdims=True) + EPS) * gout
        outs.append(o * _silu(g[:, h * DV:(h + 1) * DV]))
        u = _dot_tn(_expand(jnp.where(hm[h], k_end, 0.0), bb, ls).astype(BF16), vh)
        upd = u if upd is None else upd + u
    ones = jnp.ones((rows, LANES), F32)
    dec = jnp.exp(_dot_tn(_expand(log_a, bb, ls), ones, precision=HIGHEST))
    return jnp.concatenate(outs, axis=1), s_stack * dec + upd


def _gla_kernel(cfg, *refs):
    bb, ls, nchunk, has_s0 = cfg
    (q_ref, k_ref, v_ref, g_ref, ra_ref, wg_ref, bg_ref, gout_ref, mt_ref, mo_ref, cm_ref) = refs[:11]
    if has_s0:
        s0_ref, o_ref, s_ref = refs[11:]
    else:
        o_ref, s_ref = refs[11:]

    @pl.when(pl.program_id(2) == 0)
    def _():
        s_ref[...] = s0_ref[...] if has_s0 else jnp.zeros(s_ref.shape, F32)

    rows = bb * ls
    s = s_ref[...].reshape(bb * LANES, LANES)
    for ci in range(nchunk):
        sl = slice(ci * rows, (ci + 1) * rows)
        o, s = _gla_body(q_ref[sl, :], k_ref[sl, :], v_ref[sl, :], g_ref[sl, :], ra_ref[sl, :],
                         wg_ref[...], bg_ref[...], gout_ref[...], mt_ref[...],
                         mo_ref[...], cm_ref[...], s, bb, ls)
        o_ref[sl, :] = o.astype(o_ref.dtype)
    s_ref[...] = s.reshape(s_ref.shape)


def _mixer_grid(n_seq, t, bb, ls, nchunk):
    tl = bb * ls * nchunk
    if bb == 1:
        steps = t // tl
        grid = (n_seq, N_PAIR, steps)
        row = lambda o, j, c: o * steps + c
    else:
        assert t == ls and nchunk == 1
        grid = (n_seq // bb, N_PAIR, 1)
        row = lambda o, j, c: o
    return tl, grid, row


def _gla_call(p, wg, bg, gout, s0, *, n_seq, t, bb, ls, nchunk):
    rows_total = n_seq * t
    tl, grid, row = _mixer_grid(n_seq, t, bb, ls, nchunk)
    has_s0 = s0 is not None
    mt, mo, cm = [jnp.asarray(m) for m in _seq_consts(bb, ls, min(GLA_SUB, ls))]
    r = bb * ls

    def col(width, base):
        return pl.BlockSpec((tl, width), lambda o, j, c: (row(o, j, c), base // width + j))

    const = lambda shape: pl.BlockSpec(shape, lambda o, j, c: (0,) * len(shape))
    st_spec = pl.BlockSpec((bb, None, LANES, LANES), lambda o, j, c: (o, j, 0, 0))
    in_specs = [col(LANES, COL_QA), col(LANES, COL_KA), col(2 * DV, COL_VA), col(2 * DV, COL_GA),
                pl.BlockSpec((tl, LANES), lambda o, j, c: (row(o, j, c), COL_RA // LANES)),
                pl.BlockSpec((LANES, LANES), lambda o, j, c: (0, j)),
                pl.BlockSpec((1, LANES), lambda o, j, c: (0, j)),
                const((1, DV)), const((r, r)), const((r, r)), const((r, r))]
    args = [p, p, p, p, p, wg, bg, gout, mt, mo, cm]
    if has_s0:
        in_specs.append(st_spec)
        args.append(s0)
    return pl.pallas_call(
        functools.partial(_gla_kernel, (bb, ls, nchunk, has_s0)),
        grid=grid, in_specs=in_specs,
        out_specs=[pl.BlockSpec((tl, 2 * DV), lambda o, j, c: (row(o, j, c), j)), st_spec],
        out_shape=[jax.ShapeDtypeStruct((rows_total, N_HEADS * DV), BF16),
                   jax.ShapeDtypeStruct((n_seq, N_PAIR, LANES, LANES), F32)],
        compiler_params=_cparams(("arbitrary", "arbitrary", "arbitrary")),
        name="gla_mixer",
    )(*args)


def _ret_tables(bb, ls, pos0):
    rows = bb * ls
    log_g = np.log1p(-np.exp2(-5.0 - np.arange(N_HEADS, dtype=np.float64)))
    t = np.arange(rows)
    n = t % ls
    same_seq = (t[:, None] // ls) == (t[None, :] // ls)
    diff = (n[:, None] - n[None, :]).astype(np.float64)
    causal = same_seq & (diff >= 0)
    dmask = np.where(causal[None], np.exp(np.maximum(diff, 0.0)[None] * log_g[:, None, None]), 0.0)
    lane_head = np.arange(N_PAIR * LANES) // DK
    q_dec = np.exp((n[:, None] + 1.0) * log_g[lane_head][None])
    k_dec = np.exp((ls - 1.0 - n)[:, None] * log_g[lane_head][None])
    c_dec = np.exp(ls * log_g[lane_head])[:, None] * np.ones((1, LANES))
    c_dec = np.tile(c_dec.reshape(N_PAIR, 1, LANES, LANES), (1, bb, 1, 1)).reshape(N_PAIR, bb * LANES, LANES)
    f32 = lambda a: jnp.asarray(a, dtype=F32)
    return f32(dmask), f32(q_dec), f32(k_dec), f32(c_dec)


def _rope_tables(t, pos0):
    half = DK // 2
    inv = ROPE_BASE ** (-np.arange(half, dtype=np.float64) / half)
    ang = (np.arange(t, dtype=np.float64) + pos0)[:, None] * inv[None]
    cos = np.tile(np.cos(ang), (1, 2 * LANES // DK))
    sin = np.tile(np.concatenate([-np.sin(ang), np.sin(ang)], axis=1), (1, LANES // DK))
    return jnp.asarray(cos, dtype=F32), jnp.asarray(sin, dtype=F32)


def _rope(x, cos, sin_signed):
    lane = lax.broadcasted_iota(jnp.int32, (1, LANES), 1) % DK
    swapped = jnp.where(lane < DK // 2, pltpu.roll(x, LANES - DK // 2, 1), pltpu.roll(x, DK // 2, 1))
    return x * cos + swapped * sin_signed


def _ret_body(q, k, v, g, cos, sin, dmask, q_dec, k_dec, c_dec, gn, s_stack, bb, ls):
    hm = _head_masks()
    qr = _rope(q, cos, sin)
    kr = _rope(k, cos, sin) * (DK ** -0.5)
    kr_bf = kr.astype(BF16)
    q_in = qr * q_dec
    k_end = kr * k_dec
    s_bf = s_stack.astype(BF16)
    outs = []
    upd = None
    for h in range(2):
        vh = v[:, h * DV:(h + 1) * DV].astype(BF16)
        att = _dot_nt(jnp.where(hm[h], qr, 0.0).astype(BF16), kr_bf) * dmask[h]
        o = jnp.dot(att.astype(BF16), vh, preferred_element_type=F32)
        o = o + jnp.dot(_expand(jnp.where(hm[h], q_in, 0.0), bb, ls).astype(BF16), s_bf,
                        preferred_element_type=F32)
        d = o - jnp.mean(o, axis=-1, keepdims=True)
        var = jnp.mean(d * d, axis=-1, keepdims=True)
        o = d * lax.rsqrt(var + EPS) * gn[:, h * DV:(h + 1) * DV]
        outs.append(o * _silu(g[:, h * DV:(h + 1) * DV]))
        u = _dot_tn(_expand(jnp.where(hm[h], k_end, 0.0), bb, ls).astype(BF16), vh)
        upd = u if upd is None else upd + u
    return jnp.concatenate(outs, axis=1), s_stack * c_dec + upd


def _ret_kernel(cfg, *refs):
    bb, ls, nchunk, has_s0 = cfg
    (q_ref, k_ref, v_ref, g_ref, cos_ref, sin_ref, dm_ref, qd_ref, kd_ref, cd_ref, gn_ref) = refs[:11]
    if has_s0:
        s0_ref, o_ref, s_ref = refs[11:]
    else:
        o_ref, s_ref = refs[11:]

    @pl.when(pl.program_id(2) == 0)
    def _():
        s_ref[...] = s0_ref[...] if has_s0 else jnp.zeros(s_ref.shape, F32)

    rows = bb * ls
    s = s_ref[...].reshape(bb * LANES, LANES)
    for ci in range(nchunk):
        sl = slice(ci * rows, (ci + 1) * rows)
        o, s = _ret_body(q_ref[sl, :], k_ref[sl, :], v_ref[sl, :], g_ref[sl, :], cos_ref[sl, :],
                         sin_ref[sl, :], dm_ref[...], qd_ref[...], kd_ref[...], cd_ref[...],
                         gn_ref[...], s, bb, ls)
        o_ref[sl, :] = o.astype(o_ref.dtype)
    s_ref[...] = s.reshape(s_ref.shape)


def _ret_call(p, gn, s0, *, n_seq, t, bb, ls, nchunk, pos0):
    rows_total = n_seq * t
    tl, grid, row = _mixer_grid(n_seq, t, bb, ls, nchunk)
    has_s0 = s0 is not None
    dmask, q_dec, k_dec, c_dec = _ret_tables(bb, ls, pos0)
    cos, sin = _rope_tables(t, pos0)
    r = bb * ls
    if bb > 1:
        cos, sin = jnp.tile(cos, (bb, 1)), jnp.tile(sin, (bb, 1))
        rope_spec = pl.BlockSpec((tl, LANES), lambda o, j, c: (0, 0))
    else:
        rope_spec = pl.BlockSpec((tl, LANES), lambda o, j, c: (c, 0))

    def col(width, base):
        return pl.BlockSpec((tl, width), lambda o, j, c: (row(o, j, c), base // width + j))

    st_spec = pl.BlockSpec((bb, None, LANES, LANES), lambda o, j, c: (o, j, 0, 0))
    in_specs = [col(LANES, COL_QC), col(LANES, COL_KC), col(2 * DV, COL_VC), col(2 * DV, COL_GC),
                rope_spec, rope_spec,
                pl.BlockSpec((2, r, r), lambda o, j, c: (j, 0, 0)),
                pl.BlockSpec((r, LANES), lambda o, j, c: (0, j)),
                pl.BlockSpec((r, LANES), lambda o, j, c: (0, j)),
                pl.BlockSpec((None, bb * LANES, LANES), lambda o, j, c: (j, 0, 0)),
                pl.BlockSpec((1, 2 * DV), lambda o, j, c: (0, j))]
    args = [p, p, p, p, cos, sin, dmask, q_dec, k_dec, c_dec, gn]
    if has_s0:
        in_specs.append(st_spec)
        args.append(s0)
    return pl.pallas_call(
        functools.partial(_ret_kernel, (bb, ls, nchunk, has_s0)),
        grid=grid, in_specs=in_specs,
        out_specs=[pl.BlockSpec((tl, 2 * DV), lambda o, j, c: (row(o, j, c), j)), st_spec],
        out_shape=[jax.ShapeDtypeStruct((rows_total, N_HEADS * DV), BF16),
                   jax.ShapeDtypeStruct((n_seq, N_PAIR, LANES, LANES), F32)],
        compiler_params=_cparams(("arbitrary", "arbitrary", "arbitrary")),
        name="ret_mixer",
    )(*args)


def _gelu(x):
    return 0.5 * x * (1.0 + jnp.tanh(0.7978845608028654 * (x + 0.044715 * x * x * x)))


def _cmlp_kernel(n_tiles, want_v, u_ref, v_ref, w_ref, bias_ref, lg_ref, lb_ref, *out_refs):
    o_ref = out_refs[0]
    for ti in range(n_tiles):
        sl = slice(ti * CHUNK_B, (ti + 1) * CHUNK_B)
        u = _gelu(u_ref[sl, :])
        vv = _gelu(v_ref[sl, :])
        vn_parts, z_parts = [], []
        for gi in range(G_B):
            vg = vv[:, gi * CG_B:(gi + 1) * CG_B]
            d = vg - jnp.mean(vg, axis=-1, keepdims=True)
            vn = d * lax.rsqrt(jnp.mean(d * d, axis=-1, keepdims=True) + EPS)
            vn = vn * lg_ref[:, gi * CG_B:(gi + 1) * CG_B] + lb_ref[:, gi * CG_B:(gi + 1) * CG_B]
            vn_parts.append(vn)
            z_parts.append(jnp.dot(w_ref[gi], vn.astype(BF16), preferred_element_type=F32))
        z = jnp.concatenate(z_parts, axis=1) + bias_ref[...]
        o_ref[sl, :] = (u * z).astype(o_ref.dtype)
        if want_v:
            out_refs[1][sl, :] = jnp.concatenate(vn_parts, axis=1)


def _cmlp_call(p, w_mix, bias_tile, ln_g, ln_b, *, n_tiles, want_v):
    rows = p.shape[0]
    tl = n_tiles * CHUNK_B
    const = lambda shape: pl.BlockSpec(shape, lambda i: (0,) * len(shape))
    out_specs = [pl.BlockSpec((tl, W_B), lambda i: (i, 0))]
    out_shape = [jax.ShapeDtypeStruct((rows, W_B), BF16)]
    if want_v:
        out_specs.append(pl.BlockSpec((tl, W_B), lambda i: (i, 0)))
        out_shape.append(jax.ShapeDtypeStruct((rows, W_B), F32))
    out = pl.pallas_call(
        functools.partial(_cmlp_kernel, n_tiles, want_v),
        grid=(rows // tl,),
        in_specs=[pl.BlockSpec((tl, W_B), lambda i: (i, COL_UB // W_B)),
                  pl.BlockSpec((tl, W_B), lambda i: (i, COL_VB // W_B)),
                  const((G_B, CHUNK_B, CHUNK_B)), const((CHUNK_B, W_B)), const((1, W_B)), const((1, W_B))],
        out_specs=out_specs, out_shape=out_shape,
        compiler_params=_cparams(("arbitrary",)),
        name="cmlp_mixer",
    )(p, p, w_mix, bias_tile, ln_g, ln_b)
    return (out[0], out[1]) if want_v else (out[0], None)


def _outproj_kernel(oa_ref, ob_ref, oc_ref, w_ref, x_ref, g1_ref, gn_ref, sc_ref, sh_ref,
                    wr_ref, br_ref, xo_ref, h_ref, lg_ref):
    wa = N_HEADS * DV
    acc = jnp.dot(oa_ref[...], w_ref[0:wa, :], preferred_element_type=F32)
    acc = acc + jnp.dot(ob_ref[...], w_ref[wa:wa + W_B, :], preferred_element_type=F32)
    acc = acc + jnp.dot(oc_ref[...], w_ref[wa + W_B:, :], preferred_element_type=F32)
    x = x_ref[...] + g1_ref[...] * acc
    xo_ref[...] = x
    h = _rms_mod(x, gn_ref[...], sc_ref[...], sh_ref[...])
    h_ref[...] = _pack_bf16_pairs(h, h.shape[1] // 2)
    hi = h.astype(BF16)
    lo = (h - hi.astype(F32)).astype(BF16)
    r1 = jnp.dot(hi, wr_ref[...], preferred_element_type=F32)
    r2 = jnp.dot(lo, wr_ref[:, :LANES], preferred_element_type=F32)
    lg_ref[...] = r1[:, :LANES] + r1[:, LANES:] + r2 + br_ref[...]


def _outproj_call(oa, ob, oc, w_out, x, gate1, gnorm, scale, shift, w_router, b_router, *, tm,
                  tiles_per_group):
    rows, d = x.shape
    rspec = lambda width: pl.BlockSpec((tm, width), lambda i: (i, 0))
    const = lambda shape: pl.BlockSpec(shape, lambda i: (0,) * len(shape))
    ms = lambda m: _mod_spec(m, tiles_per_group)
    return pl.pallas_call(
        _outproj_kernel,
        grid=(rows // tm,),
        in_specs=[rspec(oa.shape[1]), rspec(ob.shape[1]), rspec(oc.shape[1]), const(w_out.shape),
                  rspec(d), ms(gate1), const((1, d)), ms(scale), ms(shift),
                  const(w_router.shape), const(b_router.shape)],
        out_specs=[rspec(d), rspec(d // 2), rspec(LANES)],
        out_shape=[jax.ShapeDtypeStruct((rows, d), F32), jax.ShapeDtypeStruct((rows, d // 2), jnp.uint32),
                   jax.ShapeDtypeStruct((rows, LANES), F32)],
        compiler_params=_cparams(("arbitrary",)),
        name="outproj_norm_router",
    )(oa, ob, oc, w_out, x, gate1, gnorm, scale, shift, w_router, b_router)


def _moe_in_kernel(se_ref, nb_ref, sx_ref, so_ref, x_ref, wg_ref, wu_ref, bg_ref, bu_ref, h_ref, x_s, wg_s, wu_s):
    nb = nb_ref[pl.program_id(0)]

    @pl.when((nb > 0) & (pl.program_id(1) == 0))
    def _():
        for rb in range(MOE_SLAB // MOE_SUB):
            sl = slice(rb * MOE_SUB, (rb + 1) * MOE_SUB)

            @pl.when(rb < nb)
            def _():
                x_s[sl, :] = _unpack_bf16_pairs(x_ref[sl, :], x_ref.shape[1]).astype(BF16)

    @pl.when(nb > 0)
    def _():
        wg_s[...] = wg_ref[...].astype(BF16)
        wu_s[...] = wu_ref[...].astype(BF16)
        for rb in range(MOE_SLAB // MOE_SUB):
            sl = slice(rb * MOE_SUB, (rb + 1) * MOE_SUB)

            @pl.when(rb < nb)
            def _():
                xb = x_s[sl, :]
                gt = jnp.dot(xb, wg_s[...], preferred_element_type=F32) + bg_ref[...]
                up = jnp.dot(xb, wu_s[...], preferred_element_type=F32) + bu_ref[...]
                gt = jnp.minimum(gt, SWIGLU_LIMIT)
                up = jnp.clip(up, -SWIGLU_LIMIT, SWIGLU_LIMIT)
                glu = gt * jax.nn.sigmoid(SWIGLU_ALPHA * gt)
                h_ref[sl, :] = ((up + 1.0) * glu).astype(h_ref.dtype)

            @pl.when(rb >= nb)
            def _():
                h_ref[sl, :] = jnp.zeros((MOE_SUB, h_ref.shape[1]), h_ref.dtype)


def _moe_out_kernel(se_ref, nb_ref, sx_ref, so_ref, h_ref, w_ref, b_ref, y_ref, w_s):
    nb = nb_ref[pl.program_id(0)]

    @pl.when(nb > 0)
    def _():
        w_s[...] = w_ref[...].astype(BF16)
        for rb in range(MOE_SLAB // MOE_SUB):
            sl = slice(rb * MOE_SUB, (rb + 1) * MOE_SUB)

            @pl.when(rb < nb)
            def _():
                y = jnp.dot(h_ref[sl, :], w_s[...], preferred_element_type=F32) + b_ref[...]
                y_ref[sl, :] = _pack_bf16_pairs(y, MOE_TN2 // 2)

            @pl.when(rb >= nb)
            def _():
                y_ref[sl, :] = jnp.zeros((MOE_SUB, y_ref.shape[1]), y_ref.dtype)


def _moe_calls(x_rows, slab_e, slab_nb, slab_x, slab_off, w_in, b_in, w_out, b_out, layer):
    n_slab = slab_e.shape[0]
    r = MOE_SLAB
    dh = x_rows.shape[1]
    d = 2 * dh
    nt1 = D_FF // MOE_TN1
    nt2 = d // MOE_TN2
    up_off = D_FF // MOE_TN1

    def jj(j, s, nb, last):
        return jnp.where(nb[s] > 0, j, last)

    def wspec(k, tn, last, off=0):
        return pl.BlockSpec((None, None, k, tn),
                            lambda s, j, se, nb, sx, so: (layer, se[s], 0, off + jj(j, s, nb, last)))

    h = pl.pallas_call(
        _moe_in_kernel,
        grid_spec=pltpu.PrefetchScalarGridSpec(
            num_scalar_prefetch=4, grid=(n_slab, nt1),
            in_specs=[
                pl.BlockSpec((pl.Element(r), pl.Element(dh)), lambda s, j, se, nb, sx, so: (so[s] * MOE_SUB, 0)),
                wspec(d, MOE_TN1, nt1 - 1), wspec(d, MOE_TN1, nt1 - 1, up_off),
                wspec(1, MOE_TN1, nt1 - 1), wspec(1, MOE_TN1, nt1 - 1, up_off),
            ],
            out_specs=pl.BlockSpec((None, r, MOE_TN1),
                                   lambda s, j, se, nb, sx, so: (sx[s], 0, jj(j, s, nb, nt1 - 1))),
            scratch_shapes=[pltpu.VMEM((r, d), BF16), pltpu.VMEM((d, MOE_TN1), BF16),
                            pltpu.VMEM((d, MOE_TN1), BF16)]),
        out_shape=jax.ShapeDtypeStruct((n_slab, r, D_FF), BF16),
        compiler_params=_cparams(("arbitrary", "arbitrary")),
        name="moe_glu",
    )(slab_e, slab_nb, slab_x, slab_off, x_rows, w_in, w_in, b_in, b_in)

    y = pl.pallas_call(
        _moe_out_kernel,
        grid_spec=pltpu.PrefetchScalarGridSpec(
            num_scalar_prefetch=4, grid=(n_slab, nt2),
            in_specs=[
                pl.BlockSpec((None, r, D_FF), lambda s, j, se, nb, sx, so: (sx[s], 0, 0)),
                wspec(D_FF, MOE_TN2, nt2 - 1), wspec(1, MOE_TN2, nt2 - 1),
            ],
            out_specs=pl.BlockSpec((None, r, MOE_TN2 // 2),
                                   lambda s, j, se, nb, sx, so: (sx[s], 0, jj(j, s, nb, nt2 - 1))),
            scratch_shapes=[pltpu.VMEM((D_FF, MOE_TN2), BF16)]),
        out_shape=jax.ShapeDtypeStruct((n_slab, r, dh), jnp.uint32),
        compiler_params=_cparams(("arbitrary", "arbitrary")),
        name="moe_down",
    )(slab_e, slab_nb, slab_x, slab_off, h, w_out, b_out)
    return y


def _rows(a, idx):
    return a.at[idx].get(mode="promise_in_bounds")


def _n_slabs(n_assign):
    return N_EXPERTS + (n_assign + N_EXPERTS * (MOE_SUB - 1)) // MOE_SLAB


def _route(logits, n_slab):
    m = logits.shape[0]
    top_val, top_idx = lax.top_k(logits, TOP_K)
    gates = jax.nn.softmax(top_val, axis=-1)
    a = m * TOP_K
    flat_e = top_idx.reshape(a).astype(jnp.int32)
    ar = jnp.arange(a, dtype=jnp.int32)
    sorted_e, order = lax.sort((flat_e, ar), num_keys=1, is_stable=True)
    counts = jnp.bincount(flat_e, length=N_EXPERTS).astype(jnp.int32)
    start = jnp.cumsum(counts) - counts
    rank = ar - start[sorted_e]
    padded = (counts + MOE_SUB - 1) // MOE_SUB * MOE_SUB
    nslab_e = (padded + MOE_SLAB - 1) // MOE_SLAB
    slab_end = jnp.cumsum(nslab_e)
    slab_start = slab_end - nslab_e
    dest = (slab_start[sorted_e] + rank // MOE_SLAB) * MOE_SLAB + rank % MOE_SLAB
    _, pos = lax.sort((order, dest), num_keys=1)
    pos = pos.reshape(m, TOP_K)
    n_used = slab_end[-1]
    sid = jnp.arange(n_slab, dtype=jnp.int32)
    used = sid < n_used
    sid_eff = jnp.minimum(sid, n_used - 1)
    slab_e = jnp.minimum(jnp.searchsorted(slab_end, sid_eff, side="right"), N_EXPERTS - 1).astype(jnp.int32)
    local = sid_eff - slab_start[slab_e]
    valid = jnp.clip(padded[slab_e] - local * MOE_SLAB, 0, MOE_SLAB)
    slab_nb = jnp.where(used, valid // MOE_SUB, 0).astype(jnp.int32)
    cend = jnp.cumsum(padded)
    cstart = cend - padded
    slab_off = ((cstart[slab_e] + local * MOE_SLAB) // MOE_SUB).astype(jnp.int32)
    n_rows = a + N_EXPERTS * MOE_SUB + MOE_SLAB
    rid = jnp.arange(n_rows, dtype=jnp.int32)
    e_r = jnp.minimum(jnp.searchsorted(cend, rid, side="right"), N_EXPERTS - 1).astype(jnp.int32)
    rk = rid - cstart[e_r]
    src = jnp.clip(start[e_r] + rk, 0, a - 1)
    row_tok = jnp.where(rk < counts[e_r], _rows(order, src) // TOP_K, rid % m)
    return gates, pos, row_tok, slab_e, slab_nb, sid_eff.astype(jnp.int32), slab_off


def _moe(h2u, logits, w_in, b_in, w_out, b_out, layer):
    m, dh = h2u.shape
    n_slab = _n_slabs(m * TOP_K)
    gates, pos, row_tok, slab_e, slab_nb, slab_x, slab_off = _route(logits, n_slab)
    x_rows = _rows(h2u, row_tok)
    y = _moe_calls(x_rows, slab_e, slab_nb, slab_x, slab_off, w_in, b_in.reshape(b_in.shape[0], N_EXPERTS, 1, -1),
                   w_out, b_out.reshape(b_out.shape[0], N_EXPERTS, 1, -1), layer)
    y4 = _rows(y.reshape(n_slab * MOE_SLAB, dh), pos.T.reshape(-1))
    return y4.reshape(TOP_K, m, dh), gates


def _prep_w_in(w_in):
    depth, d, _ = w_in.shape
    pad = jnp.zeros((depth, d, D_P - COL_RA - GLA_LOWRANK), w_in.dtype)
    w = jnp.concatenate([w_in[:, :, _O_UB:_O_QC], w_in[:, :, :_O_RA], w_in[:, :, _O_QC:_O_END],
                         w_in[:, :, _O_RA:_O_UB], pad], axis=-1)
    return w.astype(BF16)


def _prep_cmlp(w_spatial, b_spatial, t):
    tri = jnp.tril(jnp.ones((CHUNK_B, CHUNK_B), bool))
    w = jnp.where(tri[None, None], w_spatial, 0.0)
    if t < CHUNK_B:
        reps = CHUNK_B // t
        eye = jnp.eye(reps, dtype=w.dtype)
        w = jnp.einsum("ab,lgij->lgaibj", eye, w[:, :, :t, :t]).reshape(w.shape[0], G_B, CHUNK_B, CHUNK_B)
        b = jnp.tile(b_spatial[:, :, :t], (1, 1, reps))
    else:
        b = b_spatial
    bias = jnp.repeat(jnp.swapaxes(b, 1, 2), CG_B, axis=2)
    return w.astype(BF16), bias.astype(F32)


def kernel(x_prompt, x_sample, c_prompt, c_sample, state_gla, state_ret, w_ada, b_ada, g_norm1,
           g_norm2, w_in, w_gla_gate, b_gla_gate, g_gla_out, ln_g_cmlp, ln_b_cmlp, w_spatial,
           b_spatial, g_ret_out, w_out, w_router, b_router, w_e_in, b_e_in, w_e_out, b_e_out, g_final):
    bp, tp, d = x_prompt.shape
    bs, ts, _ = x_sample.shape
    rows_p, rows_s = bp * tp, bs * ts
    tmn_p, tmn_s = 256, 256
    tmo_p, tmo_s = 512, 256
    tm_in, tn_in = 1024, 1152
    dec_bb = LANES // ts

    n_c = bp + bs
    n_c_pad = -(-n_c // SUBLANES) * SUBLANES
    c_all = jnp.concatenate([c_prompt, c_sample, jnp.zeros((n_c_pad - n_c, d), F32)], axis=0)
    mod = _ada_call(c_all, w_ada, b_ada).reshape(DEPTH, n_c_pad, N_MOD, d)
    mod_p = jnp.transpose(mod[:, :bp], (0, 2, 1, 3)).reshape(DEPTH, N_MOD, bp, 1, d)
    mod_s = jnp.repeat(jnp.transpose(mod[:, bp:n_c], (0, 2, 1, 3)), ts, axis=2)

    w_in_r = _prep_w_in(w_in)
    w_out_bf = w_out.astype(BF16)
    wg_pad = jnp.pad(w_gla_gate, ((0, 0), (0, LANES - GLA_LOWRANK), (0, 0))).astype(BF16)
    w_router_pad = jnp.pad(w_router, ((0, 0), (0, 0), (0, LANES - N_EXPERTS)))
    w_router_hi = w_router_pad.astype(BF16)
    w_router_lo = (w_router_pad - w_router_hi.astype(F32)).astype(BF16)
    w_router_2 = jnp.concatenate([w_router_hi, w_router_lo], axis=-1)
    b_router_pad = jnp.pad(b_router, ((0, 0), (0, LANES - N_EXPERTS))).reshape(DEPTH, 1, LANES)
    wmix_p, bias_p = _prep_cmlp(w_spatial, b_spatial, CHUNK_B)
    wmix_s, bias_s = _prep_cmlp(w_spatial, b_spatial, ts)

    xp = x_prompt.reshape(rows_p, d)
    xs = x_sample.reshape(rows_s, d)
    st_gla = state_gla.reshape(DEPTH, bs, N_PAIR, LANES, LANES)
    st_ret = state_ret.reshape(DEPTH, bs, N_PAIR, LANES, LANES)
    y4 = gates = None
    outs = {k: [] for k in ("gla_p", "ret_p", "gla_s", "ret_s", "v_s")}

    for l in range(DEPTH):
        row = lambda a: a[l].reshape(1, -1)
        mp = lambda i: mod_p[l, i]
        msf = lambda i, tm: mod_s[l, i].reshape(rows_s // tm, tm, d)
        g2p = mod_p[l - 1, 5] if l > 0 else None
        g2s = mod_s[l - 1, 5].reshape(rows_s // tmn_s, tmn_s, d) if l > 0 else None

        xp, hp = _resid_norm_call(xp, y4, gates, g2p, row(g_norm1), mp(1), mp(0), tm=tmn_p,
                                  tiles_per_group=tp // tmn_p, row_off=0, final=False)
        xs, hs = _resid_norm_call(xs, y4, gates, g2s, row(g_norm1), msf(1, tmn_s), msf(0, tmn_s), tm=tmn_s,
                                  tiles_per_group=1, row_off=rows_p, final=False)
        pp = _inproj_call(hp, w_in_r[l], tm=tm_in, tn=tn_in)
        ps = _inproj_call(hs, w_in_r[l], tm=tm_in, tn=tn_in)

        bg = row(b_gla_gate)
        gout = row(g_gla_out)
        gret = row(g_ret_out)
        oa_p, sa_p = _gla_call(pp, wg_pad[l], bg, gout, None, n_seq=bp, t=tp, bb=1, ls=GLA_CHUNK, nchunk=4)
        oa_s, sa_s = _gla_call(ps, wg_pad[l], bg, gout, st_gla[l], n_seq=bs, t=ts, bb=dec_bb, ls=ts, nchunk=1)
        oc_p, sc_p = _ret_call(pp, gret, None, n_seq=bp, t=tp, bb=1, ls=RET_CHUNK, nchunk=2, pos0=0)
        oc_s, sc_s = _ret_call(ps, gret, st_ret[l], n_seq=bs, t=ts, bb=dec_bb, ls=ts, nchunk=1, pos0=PAST_LEN)
        ob_p, _ = _cmlp_call(pp, wmix_p[l], bias_p[l], row(ln_g_cmlp), row(ln_b_cmlp), n_tiles=4, want_v=False)
        ob_s, v_s = _cmlp_call(ps, wmix_s[l], bias_s[l], row(ln_g_cmlp), row(ln_b_cmlp), n_tiles=4, want_v=True)

        xp, h2p, lgp = _outproj_call(oa_p, ob_p, oc_p, w_out_bf[l], xp, mp(2), row(g_norm2), mp(4), mp(3),
                                     w_router_2[l], b_router_pad[l], tm=tmo_p, tiles_per_group=tp // tmo_p)
        xs, h2s, lgs = _outproj_call(oa_s, ob_s, oc_s, w_out_bf[l], xs, msf(2, tmo_s), row(g_norm2),
                                     msf(4, tmo_s), msf(3, tmo_s), w_router_2[l], b_router_pad[l],
                                     tm=tmo_s, tiles_per_group=1)

        h2 = jnp.concatenate([h2p, h2s], axis=0)
        logits = jnp.concatenate([lgp, lgs], axis=0)[:, :N_EXPERTS]
        y4, gates = _moe(h2, logits, w_e_in, b_e_in, w_e_out, b_e_out, l)

        outs["gla_p"].append(sa_p.reshape(bp, N_HEADS, DK, DV))
        outs["ret_p"].append(sc_p.reshape(bp, N_HEADS, DK, DV))
        outs["gla_s"].append(sa_s.reshape(bs, N_HEADS, DK, DV))
        outs["ret_s"].append(sc_s.reshape(bs, N_HEADS, DK, DV))
        outs["v_s"].append(v_s.reshape(bs, ts, W_B))

    yp = _resid_norm_call(xp, y4, gates, mod_p[DEPTH - 1, 5], g_final.reshape(1, d), None, None, tm=tmn_p,
                          tiles_per_group=tp // tmn_p, row_off=0, final=True)
    ys = _resid_norm_call(xs, y4, gates, mod_s[DEPTH - 1, 5].reshape(rows_s // tmn_s, tmn_s, d),
                          g_final.reshape(1, d), None, None, tm=tmn_s, tiles_per_group=1, row_off=rows_p,
                          final=True)
    return (yp.reshape(bp, tp, d), ys.reshape(bs, ts, d), jnp.stack(outs["gla_p"]), jnp.stack(outs["ret_p"]),
            jnp.stack(outs["gla_s"]), jnp.stack(outs["ret_s"]), jnp.stack(outs["v_s"]))
```

```python
import functools

import numpy as np
import jax
import jax.numpy as jnp
from jax import lax
from jax.experimental import pallas as pl
from jax.experimental.pallas import tpu as pltpu

F32 = jnp.float32
BF16 = jnp.bfloat16
HIGHEST = lax.Precision.HIGHEST

D_MODEL = 2048
DEPTH = 4
N_HEADS = 6
N_PAIR = N_HEADS // 2
DK = 64
DV = 128
GLA_LOWRANK = 16
GLA_TAU = 16.0
GLA_CHUNK = 64
GLA_SUB = 16
G_B = 4
CG_B = 128
W_B = G_B * CG_B
CHUNK_B = 128
RET_CHUNK = 128
ROPE_BASE = 10000.0
N_EXPERTS = 32
TOP_K = 4
D_FF = D_MODEL
SWIGLU_LIMIT = 7.0
SWIGLU_ALPHA = 1.702
N_MOD = 6
EPS = 1e-6
PAST_LEN = 16384

LANES = 128
SUBLANES = 8
VMEM_LIMIT = 56 * 1024 * 1024

COL_UB = 0
COL_VB = 512
COL_QA = 1024
COL_KA = 1408
COL_VA = 1792
COL_GA = 2560
COL_QC = 3328
COL_KC = 3712
COL_VC = 4096
COL_GC = 4864
COL_RA = 5632
D_P = 5760
_O_RA = 2304
_O_UB = 2320
_O_QC = 3344
_O_END = 5648

EXP_CLAMP = 80.0

MOE_SUB = 256
MOE_SLAB = 1536
MOE_TN1 = 512
MOE_TN2 = 512


def _cparams(sem):
    return pltpu.CompilerParams(dimension_semantics=sem, vmem_limit_bytes=VMEM_LIMIT)


def _ada_kernel(c_ref, w_ref, b_ref, o_ref):
    c = c_ref[...]
    s = (c * jax.nn.sigmoid(c)).astype(BF16)
    o_ref[...] = jnp.dot(s, w_ref[...].astype(BF16), preferred_element_type=F32) + b_ref[...]


def _ada_call(c_all, w_ada, b_ada, tn=1024):
    nb, d = c_all.shape
    depth, _, n = w_ada.shape
    return pl.pallas_call(
        _ada_kernel,
        grid=(depth, n // tn),
        in_specs=[
            pl.BlockSpec((nb, d), lambda l, j: (0, 0)),
            pl.BlockSpec((None, d, tn), lambda l, j: (l, 0, j)),
            pl.BlockSpec((None, 1, tn), lambda l, j: (l, 0, j)),
        ],
        out_specs=pl.BlockSpec((None, nb, tn), lambda l, j: (l, 0, j)),
        out_shape=jax.ShapeDtypeStruct((depth, nb, n), F32),
        compiler_params=_cparams(("arbitrary", "arbitrary")),
        name="ada_mod",
    )(c_all, w_ada, b_ada.reshape(depth, 1, n))


def _rms_mod(x, gn, sc, sh):
    ms = jnp.mean(x * x, axis=-1, keepdims=True)
    h = x * lax.rsqrt(ms + EPS) * gn
    return h * (1.0 + sc) + sh


_HI16 = 0xFFFF0000


def _pack_bf16_pairs(x, group):
    outs = []
    for c0 in range(0, x.shape[1], 2 * group):
        lo = lax.bitcast_convert_type(x[:, c0:c0 + group].astype(BF16).astype(F32), jnp.uint32) >> 16
        hi = lax.bitcast_convert_type(x[:, c0 + group:c0 + 2 * group].astype(BF16).astype(F32), jnp.uint32)
        outs.append((hi & jnp.uint32(_HI16)) | lo)
    return outs[0] if len(outs) == 1 else jnp.concatenate(outs, axis=1)


def _unpack_bf16_pairs(u, group):
    lo = lax.bitcast_convert_type(u << 16, F32)
    hi = lax.bitcast_convert_type(u & jnp.uint32(_HI16), F32)
    parts = []
    for c0 in range(0, u.shape[1], group):
        parts += [lo[:, c0:c0 + group], hi[:, c0:c0 + group]]
    return jnp.concatenate(parts, axis=1)


def _resid_norm_kernel(has_y, final, *refs):
    it = iter(refs)
    x_ref = next(it)
    if has_y:
        y_ref, gt_ref, g2_ref = next(it), next(it), next(it)
    gn_ref = next(it)
    if not final:
        sc_ref, sh_ref = next(it), next(it)
    if has_y and not final:
        xo_ref = next(it)
    o_ref = next(it)

    x = x_ref[...]
    if has_y:
        acc = None
        for k in range(TOP_K):
            t = gt_ref[:, k:k + 1] * _unpack_bf16_pairs(y_ref[k], MOE_TN2 // 2)
            acc = t if acc is None else acc + t
        x = x + g2_ref[...] * acc
        if not final:
            xo_ref[...] = x
    if final:
        ms = jnp.mean(x * x, axis=-1, keepdims=True)
        o_ref[...] = x * lax.rsqrt(ms + EPS) * gn_ref[...]
    else:
        o_ref[...] = _rms_mod(x, gn_ref[...], sc_ref[...], sh_ref[...]).astype(o_ref.dtype)


def _mod_spec(mod, tiles_per_group):
    _, r, d = mod.shape
    return pl.BlockSpec((None, r, d), lambda i, j=0: (i // tiles_per_group, 0, 0))


def _resid_norm_call(x, y4, gates, gate2, gnorm, scale, shift, *, tm, tiles_per_group, row_off, final):
    rows, d = x.shape
    has_y = y4 is not None
    row_spec = pl.BlockSpec((tm, d), lambda i: (i, 0))
    in_specs, args = [row_spec], [x]
    if has_y:
        off = row_off // tm
        in_specs += [pl.BlockSpec((TOP_K, tm, d // 2), lambda i: (0, i + off, 0)),
                     pl.BlockSpec((tm, TOP_K), lambda i: (i + off, 0)), _mod_spec(gate2, tiles_per_group)]
        args += [y4, gates, gate2]
    in_specs.append(pl.BlockSpec((1, d), lambda i: (0, 0)))
    args.append(gnorm)
    if not final:
        in_specs += [_mod_spec(scale, tiles_per_group), _mod_spec(shift, tiles_per_group)]
        args += [scale, shift]
    if final:
        out_specs, out_shape = row_spec, jax.ShapeDtypeStruct((rows, d), F32)
    elif has_y:
        out_specs = [row_spec, row_spec]
        out_shape = [jax.ShapeDtypeStruct((rows, d), F32), jax.ShapeDtypeStruct((rows, d), BF16)]
    else:
        out_specs, out_shape = row_spec, jax.ShapeDtypeStruct((rows, d), BF16)
    out = pl.pallas_call(
        functools.partial(_resid_norm_kernel, has_y, final),
        grid=(rows // tm,), in_specs=in_specs, out_specs=out_specs, out_shape=out_shape,
        compiler_params=_cparams(("arbitrary",)),
        name="resid_norm",
    )(*args)
    if final:
        return out
    return (out[0], out[1]) if has_y else (x, out)


def _matmul_kernel(x_ref, w_ref, o_ref):
    o_ref[...] = jnp.dot(x_ref[...], w_ref[...], preferred_element_type=F32)


def _inproj_call(h, w, layer, *, tm, tn):
    rows, d = h.shape
    n = w.shape[2]
    return pl.pallas_call(
        _matmul_kernel,
        grid=(rows // tm, n // tn),
        in_specs=[pl.BlockSpec((tm, d), lambda i, j: (i, 0)),
                  pl.BlockSpec((None, d, tn), lambda i, j: (layer, 0, j))],
        out_specs=pl.BlockSpec((tm, tn), lambda i, j: (i, j)),
        out_shape=jax.ShapeDtypeStruct((rows, n), F32),
        compiler_params=_cparams(("arbitrary", "arbitrary")),
        name="inproj",
    )(h, w)


def _head_masks(width=LANES):
    lane = lax.broadcasted_iota(jnp.int32, (1, width), 1) % LANES
    return lane < DK, lane >= DK


def _expand(x, bb, ls):
    if bb == 1:
        return x
    seq = lax.broadcasted_iota(jnp.int32, (bb * ls, 1), 0) // ls
    return jnp.concatenate([jnp.where(seq == b, x, 0.0) for b in range(bb)], axis=1)


def _dot_nt(a, b):
    return lax.dot_general(a, b, (((1,), (1,)), ((), ())), preferred_element_type=F32)


def _dot_tn(a, b, precision=None):
    return lax.dot_general(a, b, (((0,), (0,)), ((), ())), preferred_element_type=F32,
                           precision=precision)


def _silu(x):
    return x * jax.nn.sigmoid(x)


def _causal_mask(bb, ls):
    t = np.arange(bb * ls)
    same_seg = (t[:, None] // ls) == (t[None, :] // ls)
    return jnp.asarray((same_seg & (t[None, :] <= t[:, None])).astype(np.float32))


def _chain_states(s_in, dec, upd, bb):
    states = [s_in]
    for c in range(bb):
        blk = slice(c * LANES, (c + 1) * LANES)
        states.append(states[-1] * dec[blk] + upd[blk])
    s_all = states[0] if bb == 1 else jnp.concatenate(states[:bb], axis=0)
    return s_all, states[bb]


def _gla_body(q, k, v, g, ra, wg, bg, gout, cmask, s_in, bb, ls, chained):
    rows = bb * ls
    sub = min(GLA_SUB, ls)
    n_sub = ls // sub
    hm = _head_masks()
    seg = lax.broadcasted_iota(jnp.int32, (rows, 1), 0) // ls

    x = jnp.dot(ra.astype(BF16), wg, preferred_element_type=F32) + bg
    log_a = (jnp.minimum(x, 0.0) - jnp.log1p(jnp.exp(-jnp.abs(x)))) * (1.0 / GLA_TAU)
    la_hi = log_a.astype(BF16)
    la_r = log_a - la_hi.astype(F32)
    la_mid = la_r.astype(BF16)
    la_lo = (la_r - la_mid.astype(F32)).astype(BF16)
    b3 = jnp.dot(cmask.astype(BF16), jnp.concatenate([la_hi, la_mid, la_lo], axis=1),
                 preferred_element_type=F32)
    bcum = b3[:, :LANES] + b3[:, LANES:2 * LANES] + b3[:, 2 * LANES:]

    def seg_row(offset):
        out = bcum[offset:offset + 1, :]
        for b in range(1, bb):
            out = jnp.where(seg == b, bcum[b * ls + offset:b * ls + offset + 1, :], out)
        return out

    bend = seg_row(ls - 1)
    bases = [None] + [seg_row(i * sub - 1) for i in range(1, n_sub)]
    sub_id = (lax.broadcasted_iota(jnp.int32, (rows, 1), 0) % ls) // sub
    bs = bcum
    for i in range(1, n_sub):
        bs = jnp.where(sub_id == i, bcum - bases[i], bs)

    qs = q * (DK ** -0.5)
    q_sub = qs * jnp.exp(bs)
    q_seq = qs * jnp.exp(bcum)
    k_end = k * jnp.exp(bend - bcum)

    a_parts, k_parts = [], []
    for i in range(n_sub):
        shift = -bcum if i == 0 else bases[i] - bcum
        k_parts.append((k * jnp.exp(jnp.minimum(shift, EXP_CLAMP))).astype(BF16))
        a_parts.append(jnp.where(sub_id == i, q_sub, 0.0) if n_sub > 1 else q_sub)
    a = a_parts[0] if n_sub == 1 else jnp.concatenate(a_parts, axis=1)
    kk = k_parts[0] if n_sub == 1 else jnp.concatenate(k_parts, axis=1)
    hmw = _head_masks(n_sub * LANES)
    a2 = jnp.concatenate([jnp.where(hmw[0], a, 0.0), jnp.where(hmw[1], a, 0.0)], axis=0).astype(BF16)
    att = _dot_nt(a2, kk)
    att = jnp.where(jnp.concatenate([cmask, cmask], axis=0) > 0.0, att, 0.0)

    vhs = [v[:, h * DV:(h + 1) * DV].astype(BF16) for h in range(2)]
    upd = (_dot_tn(_expand(jnp.where(hm[0], k_end, 0.0), bb, ls).astype(BF16), vhs[0])
           + _dot_tn(_expand(jnp.where(hm[1], k_end, 0.0), bb, ls).astype(BF16), vhs[1]))
    dec_parts = []
    for b in range(bb):
        last = b * ls + ls - 1
        dec_parts.append(jnp.exp(jnp.broadcast_to(bcum[last:last + 1, :], (LANES, LANES)).T))
    dec = dec_parts[0] if bb == 1 else jnp.concatenate(dec_parts, axis=0)
    if chained:
        s_all, s_out = _chain_states(s_in, dec, upd, bb)
    else:
        s_all, s_out = s_in, s_in * dec + upd

    s_bf = s_all.astype(BF16)
    outs = []
    for h in range(2):
        o = jnp.dot(att[h * rows:(h + 1) * rows].astype(BF16), vhs[h], preferred_element_type=F32)
        o = o + jnp.dot(_expand(jnp.where(hm[h], q_seq, 0.0), bb, ls).astype(BF16), s_bf,
                        preferred_element_type=F32)
        o = o * lax.rsqrt(jnp.mean(o * o, axis=-1, keepdims=True) + EPS) * gout
        outs.append(o * _silu(g[:, h * DV:(h + 1) * DV]))
    return jnp.concatenate(outs, axis=1), s_out


def _gla_kernel(cfg, *refs):
    bb, ls, chained, has_s0 = cfg
    (q_ref, k_ref, v_ref, g_ref, ra_ref, wg_ref, bg_ref, gout_ref, cm_ref) = refs[:9]
    if has_s0:
        s0_ref, o_ref, s_ref = refs[9:]
    else:
        o_ref, s_ref = refs[9:]

    @pl.when(pl.program_id(2) == 0)
    def _():
        s_ref[...] = s0_ref[...] if has_s0 else jnp.zeros(s_ref.shape, F32)

    s = s_ref[...].reshape(s_ref.shape[0] * LANES, LANES)
    o, s = _gla_body(q_ref[...], k_ref[...], v_ref[...], g_ref[...], ra_ref[...], wg_ref[...], bg_ref[...],
                     gout_ref[...], cm_ref[...], s, bb, ls, chained)
    o_ref[...] = o.astype(o_ref.dtype)
    s_ref[...] = s.reshape(s_ref.shape)


def _mixer_grid(n_seq, t, bb, ls, chained):
    tl = bb * ls
    if chained:
        steps = t // tl
        return tl, (n_seq, N_PAIR, steps), (lambda o, j, c: o * steps + c), 1
    assert t == ls
    return tl, (n_seq // bb, N_PAIR, 1), (lambda o, j, c: o), bb


def _gla_call(p, wg, bg, gout, s0, *, n_seq, t, bb, ls, chained):
    rows_total = n_seq * t
    tl, grid, row, sb = _mixer_grid(n_seq, t, bb, ls, chained)
    has_s0 = s0 is not None
    cm = _causal_mask(bb, ls)
    r = tl

    def col(width, base):
        return pl.BlockSpec((tl, width), lambda o, j, c: (row(o, j, c), base // width + j))

    const = lambda shape: pl.BlockSpec(shape, lambda o, j, c: (0,) * len(shape))
    st_spec = pl.BlockSpec((sb, None, LANES, LANES), lambda o, j, c: (o, j, 0, 0))
    in_specs = [col(LANES, COL_QA), col(LANES, COL_KA), col(2 * DV, COL_VA), col(2 * DV, COL_GA),
                pl.BlockSpec((tl, LANES), lambda o, j, c: (row(o, j, c), COL_RA // LANES)),
                pl.BlockSpec((LANES, LANES), lambda o, j, c: (0, j)),
                pl.BlockSpec((1, LANES), lambda o, j, c: (0, j)),
                const((1, DV)), const((r, r))]
    args = [p, p, p, p, p, wg, bg, gout, cm]
    if has_s0:
        in_specs.append(st_spec)
        args.append(s0)
    return pl.pallas_call(
        functools.partial(_gla_kernel, (bb, ls, chained, has_s0)),
        grid=grid, in_specs=in_specs,
        out_specs=[pl.BlockSpec((tl, 2 * DV), lambda o, j, c: (row(o, j, c), j)), st_spec],
        out_shape=[jax.ShapeDtypeStruct((rows_total, N_HEADS * DV), BF16),
                   jax.ShapeDtypeStruct((n_seq, N_PAIR, LANES, LANES), F32)],
        compiler_params=_cparams(("arbitrary", "arbitrary", "arbitrary")),
        name="gla_mixer",
    )(*args)


def _ret_tables(bb, ls, pos0):
    rows = bb * ls
    log_g = np.log1p(-np.exp2(-5.0 - np.arange(N_HEADS, dtype=np.float64)))
    t = np.arange(rows)
    n = t % ls
    same_seq = (t[:, None] // ls) == (t[None, :] // ls)
    diff = (n[:, None] - n[None, :]).astype(np.float64)
    causal = same_seq & (diff >= 0)
    dmask = np.where(causal[None], np.exp(np.maximum(diff, 0.0)[None] * log_g[:, None, None]), 0.0)
    lane_head = np.arange(N_PAIR * LANES) // DK
    q_dec = np.exp((n[:, None] + 1.0) * log_g[lane_head][None])
    k_dec = np.exp((ls - 1.0 - n)[:, None] * log_g[lane_head][None])
    c_dec = np.exp(ls * log_g[lane_head])[:, None] * np.ones((1, LANES))
    c_dec = np.tile(c_dec.reshape(N_PAIR, 1, LANES, LANES), (1, bb, 1, 1)).reshape(N_PAIR, bb * LANES, LANES)
    f32 = lambda a: jnp.asarray(a, dtype=F32)
    return f32(dmask), f32(q_dec), f32(k_dec), f32(c_dec)


def _rope_tables(t, pos0):
    half = DK // 2
    inv = ROPE_BASE ** (-np.arange(half, dtype=np.float64) / half)
    ang = (np.arange(t, dtype=np.float64) + pos0)[:, None] * inv[None]
    cos = np.tile(np.cos(ang), (1, 2 * LANES // DK))
    sin = np.tile(np.concatenate([-np.sin(ang), np.sin(ang)], axis=1), (1, LANES // DK))
    return jnp.asarray(cos, dtype=F32), jnp.asarray(sin, dtype=F32)


def _rope(x, cos, sin_signed):
    lane = lax.broadcasted_iota(jnp.int32, (1, LANES), 1) % DK
    swapped = jnp.where(lane < DK // 2, pltpu.roll(x, LANES - DK // 2, 1), pltpu.roll(x, DK // 2, 1))
    return x * cos + swapped * sin_signed


def _ret_body(q, k, v, g, cos, sin, dmask, q_dec, k_dec, c_dec, gn, s_in, bb, ls, chained):
    hm = _head_masks()
    qr = _rope(q, cos, sin)
    kr = _rope(k, cos, sin) * (DK ** -0.5)
    kr_bf = kr.astype(BF16)
    q_in = qr * q_dec
    k_end = kr * k_dec
    vhs = [v[:, h * DV:(h + 1) * DV].astype(BF16) for h in range(2)]
    upd = (_dot_tn(_expand(jnp.where(hm[0], k_end, 0.0), bb, ls).astype(BF16), vhs[0])
           + _dot_tn(_expand(jnp.where(hm[1], k_end, 0.0), bb, ls).astype(BF16), vhs[1]))
    if chained:
        s_all, s_out = _chain_states(s_in, c_dec, upd, bb)
    else:
        s_all, s_out = s_in, s_in * c_dec + upd
    s_bf = s_all.astype(BF16)
    outs = []
    for h in range(2):
        att = _dot_nt(jnp.where(hm[h], qr, 0.0).astype(BF16), kr_bf) * dmask[h]
        o = jnp.dot(att.astype(BF16), vhs[h], preferred_element_type=F32)
        o = o + jnp.dot(_expand(jnp.where(hm[h], q_in, 0.0), bb, ls).astype(BF16), s_bf,
                        preferred_element_type=F32)
        d = o - jnp.mean(o, axis=-1, keepdims=True)
        var = jnp.mean(d * d, axis=-1, keepdims=True)
        o = d * lax.rsqrt(var + EPS) * gn[:, h * DV:(h + 1) * DV]
        outs.append(o * _silu(g[:, h * DV:(h + 1) * DV]))
    return jnp.concatenate(outs, axis=1), s_out


def _ret_kernel(cfg, *refs):
    bb, ls, chained, has_s0 = cfg
    (q_ref, k_ref, v_ref, g_ref, cos_ref, sin_ref, dm_ref, qd_ref, kd_ref, cd_ref, gn_ref) = refs[:11]
    if has_s0:
        s0_ref, o_ref, s_ref = refs[11:]
    else:
        o_ref, s_ref = refs[11:]

    @pl.when(pl.program_id(2) == 0)
    def _():
        s_ref[...] = s0_ref[...] if has_s0 else jnp.zeros(s_ref.shape, F32)

    s = s_ref[...].reshape(s_ref.shape[0] * LANES, LANES)
    o, s = _ret_body(q_ref[...], k_ref[...], v_ref[...], g_ref[...], cos_ref[...], sin_ref[...], dm_ref[...],
                     qd_ref[...], kd_ref[...], cd_ref[...], gn_ref[...], s, bb, ls, chained)
    o_ref[...] = o.astype(o_ref.dtype)
    s_ref[...] = s.reshape(s_ref.shape)


def _ret_call(p, gn, s0, *, n_seq, t, bb, ls, chained, pos0):
    rows_total = n_seq * t
    tl, grid, row, sb = _mixer_grid(n_seq, t, bb, ls, chained)
    has_s0 = s0 is not None
    dmask, q_dec, k_dec, c_dec = _ret_tables(bb, ls, pos0)
    cos, sin = _rope_tables(t, pos0)
    r = tl
    if chained:
        rope_spec = pl.BlockSpec((tl, LANES), lambda o, j, c: (c, 0))
    else:
        cos, sin = jnp.tile(cos, (bb, 1)), jnp.tile(sin, (bb, 1))
        rope_spec = pl.BlockSpec((tl, LANES), lambda o, j, c: (0, 0))

    def col(width, base):
        return pl.BlockSpec((tl, width), lambda o, j, c: (row(o, j, c), base // width + j))

    st_spec = pl.BlockSpec((sb, None, LANES, LANES), lambda o, j, c: (o, j, 0, 0))
    in_specs = [col(LANES, COL_QC), col(LANES, COL_KC), col(2 * DV, COL_VC), col(2 * DV, COL_GC),
                rope_spec, rope_spec,
                pl.BlockSpec((2, r, r), lambda o, j, c: (j, 0, 0)),
                pl.BlockSpec((r, LANES), lambda o, j, c: (0, j)),
                pl.BlockSpec((r, LANES), lambda o, j, c: (0, j)),
                pl.BlockSpec((None, bb * LANES, LANES), lambda o, j, c: (j, 0, 0)),
                pl.BlockSpec((1, 2 * DV), lambda o, j, c: (0, j))]
    args = [p, p, p, p, cos, sin, dmask, q_dec, k_dec, c_dec, gn]
    if has_s0:
        in_specs.append(st_spec)
        args.append(s0)
    return pl.pallas_call(
        functools.partial(_ret_kernel, (bb, ls, chained, has_s0)),
        grid=grid, in_specs=in_specs,
        out_specs=[pl.BlockSpec((tl, 2 * DV), lambda o, j, c: (row(o, j, c), j)), st_spec],
        out_shape=[jax.ShapeDtypeStruct((rows_total, N_HEADS * DV), BF16),
                   jax.ShapeDtypeStruct((n_seq, N_PAIR, LANES, LANES), F32)],
        compiler_params=_cparams(("arbitrary", "arbitrary", "arbitrary")),
        name="ret_mixer",
    )(*args)


def _gelu(x):
    return 0.5 * x * (1.0 + jnp.tanh(0.7978845608028654 * (x + 0.044715 * x * x * x)))


def _cmlp_kernel(n_tiles, want_v, u_ref, v_ref, w_ref, bias_ref, lg_ref, lb_ref, *out_refs):
    o_ref = out_refs[0]
    for ti in range(n_tiles):
        sl = slice(ti * CHUNK_B, (ti + 1) * CHUNK_B)
        u = _gelu(u_ref[sl, :])
        vv = _gelu(v_ref[sl, :])
        vn_parts, z_parts = [], []
        for gi in range(G_B):
            vg = vv[:, gi * CG_B:(gi + 1) * CG_B]
            d = vg - jnp.mean(vg, axis=-1, keepdims=True)
            vn = d * lax.rsqrt(jnp.mean(d * d, axis=-1, keepdims=True) + EPS)
            vn = vn * lg_ref[:, gi * CG_B:(gi + 1) * CG_B] + lb_ref[:, gi * CG_B:(gi + 1) * CG_B]
            vn_parts.append(vn)
            z_parts.append(jnp.dot(w_ref[gi], vn.astype(BF16), preferred_element_type=F32))
        z = jnp.concatenate(z_parts, axis=1) + bias_ref[...]
        o_ref[sl, :] = (u * z).astype(o_ref.dtype)
        if want_v:
            out_refs[1][sl, :] = jnp.concatenate(vn_parts, axis=1)


def _cmlp_call(p, w_mix, bias_tile, ln_g, ln_b, *, n_tiles, want_v):
    rows = p.shape[0]
    tl = n_tiles * CHUNK_B
    const = lambda shape: pl.BlockSpec(shape, lambda i: (0,) * len(shape))
    out_specs = [pl.BlockSpec((tl, W_B), lambda i: (i, 0))]
    out_shape = [jax.ShapeDtypeStruct((rows, W_B), BF16)]
    if want_v:
        out_specs.append(pl.BlockSpec((tl, W_B), lambda i: (i, 0)))
        out_shape.append(jax.ShapeDtypeStruct((rows, W_B), F32))
    out = pl.pallas_call(
        functools.partial(_cmlp_kernel, n_tiles, want_v),
        grid=(rows // tl,),
        in_specs=[pl.BlockSpec((tl, W_B), lambda i: (i, COL_UB // W_B)),
                  pl.BlockSpec((tl, W_B), lambda i: (i, COL_VB // W_B)),
                  const((G_B, CHUNK_B, CHUNK_B)), const((CHUNK_B, W_B)), const((1, W_B)), const((1, W_B))],
        out_specs=out_specs, out_shape=out_shape,
        compiler_params=_cparams(("arbitrary",)),
        name="cmlp_mixer",
    )(p, p, w_mix, bias_tile, ln_g, ln_b)
    return (out[0], out[1]) if want_v else (out[0], None)


def _outproj_kernel(oa_ref, ob_ref, oc_ref, w_ref, x_ref, g1_ref, gn_ref, sc_ref, sh_ref,
                    wr_ref, br_ref, xo_ref, h_ref, lg_ref):
    wa = N_HEADS * DV
    acc = jnp.dot(oa_ref[...], w_ref[0:wa, :], preferred_element_type=F32)
    acc = acc + jnp.dot(ob_ref[...], w_ref[wa:wa + W_B, :], preferred_element_type=F32)
    acc = acc + jnp.dot(oc_ref[...], w_ref[wa + W_B:, :], preferred_element_type=F32)
    x = x_ref[...] + g1_ref[...] * acc
    xo_ref[...] = x
    h = _rms_mod(x, gn_ref[...], sc_ref[...], sh_ref[...])
    h_ref[...] = _pack_bf16_pairs(h, h.shape[1] // 2)
    hi = h.astype(BF16)
    lo = (h - hi.astype(F32)).astype(BF16)
    r1 = jnp.dot(hi, wr_ref[...], preferred_element_type=F32)
    r2 = jnp.dot(lo, wr_ref[:, :LANES], preferred_element_type=F32)
    lg_ref[...] = r1[:, :LANES] + r1[:, LANES:] + r2 + br_ref[...]


def _outproj_call(oa, ob, oc, w_out, layer, x, gate1, gnorm, scale, shift, w_router, b_router, *, tm,
                  tiles_per_group):
    rows, d = x.shape
    rspec = lambda width: pl.BlockSpec((tm, width), lambda i: (i, 0))
    const = lambda shape: pl.BlockSpec(shape, lambda i: (0,) * len(shape))
    ms = lambda m: _mod_spec(m, tiles_per_group)
    return pl.pallas_call(
        _outproj_kernel,
        grid=(rows // tm,),
        in_specs=[rspec(oa.shape[1]), rspec(ob.shape[1]), rspec(oc.shape[1]),
                  pl.BlockSpec((None,) + w_out.shape[1:], lambda i: (layer, 0, 0)),
                  rspec(d), ms(gate1), const((1, d)), ms(scale), ms(shift),
                  const(w_router.shape), const(b_router.shape)],
        out_specs=[rspec(d), rspec(d // 2), rspec(LANES)],
        out_shape=[jax.ShapeDtypeStruct((rows, d), F32), jax.ShapeDtypeStruct((rows, d // 2), jnp.uint32),
                   jax.ShapeDtypeStruct((rows, LANES), F32)],
        compiler_params=_cparams(("arbitrary",)),
        name="outproj_norm_router",
    )(oa, ob, oc, w_out, x, gate1, gnorm, scale, shift, w_router, b_router)


def _moe_in_kernel(se_ref, nb_ref, so_ref, x_ref, wg_ref, wu_ref, bg_ref, bu_ref, h_ref, x_s, wg_s, wu_s):
    nb = nb_ref[pl.program_id(0)]

    @pl.when(pl.program_id(1) == 0)
    def _():
        for rb in range(MOE_SLAB // MOE_SUB):
            sl = slice(rb * MOE_SUB, (rb + 1) * MOE_SUB)

            @pl.when(rb < nb)
            def _():
                x_s[sl, :] = _unpack_bf16_pairs(x_ref[sl, :], x_ref.shape[1]).astype(BF16)

    wg_s[...] = wg_ref[...].astype(BF16)
    wu_s[...] = wu_ref[...].astype(BF16)
    for rb in range(MOE_SLAB // MOE_SUB):
        sl = slice(rb * MOE_SUB, (rb + 1) * MOE_SUB)

        @pl.when(rb < nb)
        def _():
            xb = x_s[sl, :]
            gt = jnp.dot(xb, wg_s[...], preferred_element_type=F32) + bg_ref[...]
            up = jnp.dot(xb, wu_s[...], preferred_element_type=F32) + bu_ref[...]
            gt = jnp.minimum(gt, SWIGLU_LIMIT)
            up = jnp.clip(up, -SWIGLU_LIMIT, SWIGLU_LIMIT)
            glu = gt * jax.nn.sigmoid(SWIGLU_ALPHA * gt)
            h_ref[sl, :] = ((up + 1.0) * glu).astype(h_ref.dtype)

        @pl.when(rb >= nb)
        def _():
            h_ref[sl, :] = jnp.zeros((MOE_SUB, h_ref.shape[1]), h_ref.dtype)


def _moe_out_kernel(se_ref, nb_ref, so_ref, h_ref, w_ref, b_ref, y_ref, w_s):
    nb = nb_ref[pl.program_id(0)]
    w_s[...] = w_ref[...].astype(BF16)
    for rb in range(MOE_SLAB // MOE_SUB):
        sl = slice(rb * MOE_SUB, (rb + 1) * MOE_SUB)

        @pl.when(rb < nb)
        def _():
            y = jnp.dot(h_ref[sl, :], w_s[...], preferred_element_type=F32) + b_ref[...]
            y_ref[sl, :] = _pack_bf16_pairs(y, MOE_TN2 // 2)

        @pl.when(rb >= nb)
        def _():
            y_ref[sl, :] = jnp.zeros((MOE_SUB, y_ref.shape[1]), y_ref.dtype)


def _moe_calls(x_rows, n_used, slab_e, slab_nb, slab_off, w_in, b_in, w_out, b_out, layer):
    n_slab = slab_e.shape[0]
    r = MOE_SLAB
    dh = x_rows.shape[1]
    d = 2 * dh
    nt1 = D_FF // MOE_TN1
    nt2 = d // MOE_TN2
    up_off = D_FF // MOE_TN1

    def wspec(k, tn, off=0):
        return pl.BlockSpec((None, None, k, tn), lambda s, j, se, nb, so: (layer, se[s], 0, off + j))

    h = pl.pallas_call(
        _moe_in_kernel,
        grid_spec=pltpu.PrefetchScalarGridSpec(
            num_scalar_prefetch=3, grid=(n_used, nt1),
            in_specs=[
                pl.BlockSpec((pl.Element(r), pl.Element(dh)), lambda s, j, se, nb, so: (so[s] * MOE_SUB, 0)),
                wspec(d, MOE_TN1), wspec(d, MOE_TN1, up_off), wspec(1, MOE_TN1), wspec(1, MOE_TN1, up_off),
            ],
            out_specs=pl.BlockSpec((None, r, MOE_TN1), lambda s, j, se, nb, so: (s, 0, j)),
            scratch_shapes=[pltpu.VMEM((r, d), BF16), pltpu.VMEM((d, MOE_TN1), BF16),
                            pltpu.VMEM((d, MOE_TN1), BF16)]),
        out_shape=jax.ShapeDtypeStruct((n_slab, r, D_FF), BF16),
        compiler_params=_cparams(("arbitrary", "arbitrary")),
        name="moe_glu",
    )(slab_e, slab_nb, slab_off, x_rows, w_in, w_in, b_in, b_in)

    y = pl.pallas_call(
        _moe_out_kernel,
        grid_spec=pltpu.PrefetchScalarGridSpec(
            num_scalar_prefetch=3, grid=(n_used, nt2),
            in_specs=[
                pl.BlockSpec((None, r, D_FF), lambda s, j, se, nb, so: (s, 0, 0)),
                wspec(D_FF, MOE_TN2), wspec(1, MOE_TN2),
            ],
            out_specs=pl.BlockSpec((None, r, MOE_TN2 // 2), lambda s, j, se, nb, so: (s, 0, j)),
            scratch_shapes=[pltpu.VMEM((D_FF, MOE_TN2), BF16)]),
        out_shape=jax.ShapeDtypeStruct((n_slab, r, dh), jnp.uint32),
        compiler_params=_cparams(("arbitrary", "arbitrary")),
        name="moe_down",
    )(slab_e, slab_nb, slab_off, h, w_out, b_out)
    return y


def _rows(a, idx):
    return a.at[idx].get(mode="promise_in_bounds")


def _count_le(bounds, x):
    return jnp.sum((bounds <= x[..., None]).astype(jnp.int32), axis=-1)


def _n_slabs(n_assign):
    return N_EXPERTS + (n_assign + N_EXPERTS * (MOE_SUB - 1)) // MOE_SLAB


def _route(logits, n_slab):
    m = logits.shape[0]
    top_val, top_idx = lax.top_k(logits, TOP_K)
    gates = jax.nn.softmax(top_val, axis=-1)
    a = m * TOP_K
    flat_e = top_idx.reshape(a).astype(jnp.int32)
    ar = jnp.arange(a, dtype=jnp.int32)
    sorted_e, order = lax.sort((flat_e, ar), num_keys=1, is_stable=True)
    counts = jnp.bincount(flat_e, length=N_EXPERTS).astype(jnp.int32)
    start = jnp.cumsum(counts) - counts
    rank = ar - start[sorted_e]
    padded = (counts + MOE_SUB - 1) // MOE_SUB * MOE_SUB
    nslab_e = (padded + MOE_SLAB - 1) // MOE_SLAB
    slab_end = jnp.cumsum(nslab_e)
    slab_start = slab_end - nslab_e
    dest = (slab_start[sorted_e] + rank // MOE_SLAB) * MOE_SLAB + rank % MOE_SLAB
    _, pos = lax.sort((order, dest), num_keys=1)
    pos = pos.reshape(m, TOP_K)
    n_used = slab_end[-1]
    sid = jnp.arange(n_slab, dtype=jnp.int32)
    slab_e = jnp.minimum(_count_le(slab_end, sid), N_EXPERTS - 1)
    local = sid - slab_start[slab_e]
    slab_nb = (jnp.clip(padded[slab_e] - local * MOE_SLAB, 0, MOE_SLAB) // MOE_SUB).astype(jnp.int32)
    cend = jnp.cumsum(padded)
    cstart = cend - padded
    slab_off = ((cstart[slab_e] + local * MOE_SLAB) // MOE_SUB).astype(jnp.int32)
    n_rows = a + N_EXPERTS * MOE_SUB + MOE_SLAB
    rid = jnp.arange(n_rows, dtype=jnp.int32)
    e_r = jnp.minimum(_count_le(cend, rid), N_EXPERTS - 1)
    rk = rid - cstart[e_r]
    src = jnp.clip(start[e_r] + rk, 0, a - 1)
    row_tok = jnp.where(rk < counts[e_r], _rows(order, src) // TOP_K, rid % m)
    return gates, pos, row_tok, n_used.astype(jnp.int32), slab_e, slab_nb, slab_off


def _moe(h2u, logits, w_in, b_in, w_out, b_out, layer):
    m, dh = h2u.shape
    n_slab = _n_slabs(m * TOP_K)
    gates, pos, row_tok, n_used, slab_e, slab_nb, slab_off = _route(logits, n_slab)
    x_rows = _rows(h2u, row_tok)
    y = _moe_calls(x_rows, n_used, slab_e, slab_nb, slab_off, w_in, b_in.reshape(b_in.shape[0], N_EXPERTS, 1, -1),
                   w_out, b_out.reshape(b_out.shape[0], N_EXPERTS, 1, -1), layer)
    y4 = _rows(y.reshape(n_slab * MOE_SLAB, dh), pos.T.reshape(-1))
    return y4.reshape(TOP_K, m, dh), gates


def _prep_w_in(w_in):
    depth, d, _ = w_in.shape
    w = w_in.astype(BF16)
    pad = jnp.zeros((depth, d, D_P - COL_RA - GLA_LOWRANK), BF16)
    return jnp.concatenate([w[:, :, _O_UB:_O_QC], w[:, :, :_O_RA], w[:, :, _O_QC:_O_END],
                            w[:, :, _O_RA:_O_UB], pad], axis=-1)


def _prep_cmlp(w_spatial, b_spatial, t):
    tri = jnp.tril(jnp.ones((CHUNK_B, CHUNK_B), bool))
    w = jnp.where(tri[None, None], w_spatial, 0.0)
    if t < CHUNK_B:
        reps = CHUNK_B // t
        eye = jnp.eye(reps, dtype=w.dtype)
        w = jnp.einsum("ab,lgij->lgaibj", eye, w[:, :, :t, :t]).reshape(w.shape[0], G_B, CHUNK_B, CHUNK_B)
        b = jnp.tile(b_spatial[:, :, :t], (1, 1, reps))
    else:
        b = b_spatial
    bias = jnp.repeat(jnp.swapaxes(b, 1, 2), CG_B, axis=2)
    return w.astype(BF16), bias.astype(F32)


def kernel(x_prompt, x_sample, c_prompt, c_sample, state_gla, state_ret, w_ada, b_ada, g_norm1,
           g_norm2, w_in, w_gla_gate, b_gla_gate, g_gla_out, ln_g_cmlp, ln_b_cmlp, w_spatial,
           b_spatial, g_ret_out, w_out, w_router, b_router, w_e_in, b_e_in, w_e_out, b_e_out, g_final):
    bp, tp, d = x_prompt.shape
    bs, ts, _ = x_sample.shape
    rows_p, rows_s = bp * tp, bs * ts
    tmn_p, tmn_s = 256, 256
    tmo_p, tmo_s = 512, 256
    tm_in, tn_in = 1024, 1152
    dec_bb = LANES // ts

    n_c = bp + bs
    n_c_pad = -(-n_c // SUBLANES) * SUBLANES
    c_all = jnp.concatenate([c_prompt, c_sample, jnp.zeros((n_c_pad - n_c, d), F32)], axis=0)
    mod = _ada_call(c_all, w_ada, b_ada).reshape(DEPTH, n_c_pad, N_MOD, d)
    mod_p = jnp.transpose(mod[:, :bp], (0, 2, 1, 3)).reshape(DEPTH, N_MOD, bp, 1, d)
    mod_s = jnp.repeat(jnp.transpose(mod[:, bp:n_c], (0, 2, 1, 3)), ts, axis=2)

    w_in_r = _prep_w_in(w_in)
    w_out_bf = w_out.astype(BF16)
    wg_pad = jnp.pad(w_gla_gate, ((0, 0), (0, LANES - GLA_LOWRANK), (0, 0))).astype(BF16)
    w_router_pad = jnp.pad(w_router, ((0, 0), (0, 0), (0, LANES - N_EXPERTS)))
    w_router_hi = w_router_pad.astype(BF16)
    w_router_lo = (w_router_pad - w_router_hi.astype(F32)).astype(BF16)
    w_router_2 = jnp.concatenate([w_router_hi, w_router_lo], axis=-1)
    b_router_pad = jnp.pad(b_router, ((0, 0), (0, LANES - N_EXPERTS))).reshape(DEPTH, 1, LANES)
    wmix_p, bias_p = _prep_cmlp(w_spatial, b_spatial, CHUNK_B)
    wmix_s, bias_s = _prep_cmlp(w_spatial, b_spatial, ts)

    xp = x_prompt.reshape(rows_p, d)
    xs = x_sample.reshape(rows_s, d)
    st_gla = state_gla.reshape(DEPTH, bs, N_PAIR, LANES, LANES)
    st_ret = state_ret.reshape(DEPTH, bs, N_PAIR, LANES, LANES)
    y4 = gates = None
    outs = {k: [] for k in ("gla_p", "ret_p", "gla_s", "ret_s", "v_s")}

    for l in range(DEPTH):
        row = lambda a: a[l].reshape(1, -1)
        mp = lambda i: mod_p[l, i]
        msf = lambda i, tm: mod_s[l, i].reshape(rows_s // tm, tm, d)
        g2p = mod_p[l - 1, 5] if l > 0 else None
        g2s = mod_s[l - 1, 5].reshape(rows_s // tmn_s, tmn_s, d) if l > 0 else None

        xp, hp = _resid_norm_call(xp, y4, gates, g2p, row(g_norm1), mp(1), mp(0), tm=tmn_p,
                                  tiles_per_group=tp // tmn_p, row_off=0, final=False)
        xs, hs = _resid_norm_call(xs, y4, gates, g2s, row(g_norm1), msf(1, tmn_s), msf(0, tmn_s), tm=tmn_s,
                                  tiles_per_group=1, row_off=rows_p, final=False)
        pp = _inproj_call(hp, w_in_r, l, tm=tm_in, tn=tn_in)
        ps = _inproj_call(hs, w_in_r, l, tm=tm_in, tn=tn_in)

        bg = row(b_gla_gate)
        gout = row(g_gla_out)
        gret = row(g_ret_out)
        oa_p, sa_p = _gla_call(pp, wg_pad[l], bg, gout, None, n_seq=bp, t=tp, bb=4, ls=GLA_CHUNK, chained=True)
        oa_s, sa_s = _gla_call(ps, wg_pad[l], bg, gout, st_gla[l], n_seq=bs, t=ts, bb=dec_bb, ls=ts,
                               chained=False)
        oc_p, sc_p = _ret_call(pp, gret, None, n_seq=bp, t=tp, bb=2, ls=RET_CHUNK, chained=True, pos0=0)
        oc_s, sc_s = _ret_call(ps, gret, st_ret[l], n_seq=bs, t=ts, bb=dec_bb, ls=ts, chained=False,
                               pos0=PAST_LEN)
        ob_p, _ = _cmlp_call(pp, wmix_p[l], bias_p[l], row(ln_g_cmlp), row(ln_b_cmlp), n_tiles=4, want_v=False)
        ob_s, v_s = _cmlp_call(ps, wmix_s[l], bias_s[l], row(ln_g_cmlp), row(ln_b_cmlp), n_tiles=4, want_v=True)

        xp, h2p, lgp = _outproj_call(oa_p, ob_p, oc_p, w_out_bf, l, xp, mp(2), row(g_norm2), mp(4), mp(3),
                                     w_router_2[l], b_router_pad[l], tm=tmo_p, tiles_per_group=tp // tmo_p)
        xs, h2s, lgs = _outproj_call(oa_s, ob_s, oc_s, w_out_bf, l, xs, msf(2, tmo_s), row(g_norm2),
                                     msf(4, tmo_s), msf(3, tmo_s), w_router_2[l], b_router_pad[l],
                                     tm=tmo_s, tiles_per_group=1)

        h2 = jnp.concatenate([h2p, h2s], axis=0)
        logits = jnp.concatenate([lgp, lgs], axis=0)[:, :N_EXPERTS]
        y4, gates = _moe(h2, logits, w_e_in, b_e_in, w_e_out, b_e_out, l)

        outs["gla_p"].append(sa_p.reshape(bp, N_HEADS, DK, DV))
        outs["ret_p"].append(sc_p.reshape(bp, N_HEADS, DK, DV))
        outs["gla_s"].append(sa_s.reshape(bs, N_HEADS, DK, DV))
        outs["ret_s"].append(sc_s.reshape(bs, N_HEADS, DK, DV))
        outs["v_s"].append(v_s.reshape(bs, ts, W_B))

    yp = _resid_norm_call(xp, y4, gates, mod_p[DEPTH - 1, 5], g_final.reshape(1, d), None, None, tm=tmn_p,
                          tiles_per_group=tp // tmn_p, row_off=0, final=True)
    ys = _resid_norm_call(xs, y4, gates, mod_s[DEPTH - 1, 5].reshape(rows_s // tmn_s, tmn_s, d),
                          g_final.reshape(1, d), None, None, tm=tmn_s, tiles_per_group=1, row_off=rows_p,
                          final=True)
    return (yp.reshape(bp, tp, d), ys.reshape(bs, ts, d), jnp.stack(outs["gla_p"]), jnp.stack(outs["ret_p"]),
            jnp.stack(outs["gla_s"]), jnp.stack(outs["ret_s"]), jnp.stack(outs["v_s"]))
```

```python
import functools

import numpy as np
import jax
import jax.numpy as jnp
from jax import lax
from jax.experimental import pallas as pl
from jax.experimental.pallas import tpu as pltpu

F32 = jnp.float32
BF16 = jnp.bfloat16
HIGHEST = lax.Precision.HIGHEST

D_MODEL = 2048
DEPTH = 4
N_HEADS = 6
N_PAIR = N_HEADS // 2
DK = 64
DV = 128
GLA_LOWRANK = 16
GLA_TAU = 16.0
GLA_CHUNK = 64
GLA_SUB = 16
G_B = 4
CG_B = 128
W_B = G_B * CG_B
CHUNK_B = 128
RET_CHUNK = 128
ROPE_BASE = 10000.0
N_EXPERTS = 32
TOP_K = 4
D_FF = D_MODEL
SWIGLU_LIMIT = 7.0
SWIGLU_ALPHA = 1.702
N_MOD = 6
EPS = 1e-6
PAST_LEN = 16384

LANES = 128
SUBLANES = 8
VMEM_LIMIT = 56 * 1024 * 1024

COL_UB = 0
COL_VB = 512
COL_QA = 1024
COL_KA = 1408
COL_VA = 1792
COL_GA = 2560
COL_QC = 3328
COL_KC = 3712
COL_VC = 4096
COL_GC = 4864
COL_RA = 5632
D_P = 5760
_O_RA = 2304
_O_UB = 2320
_O_QC = 3344
_O_END = 5648

EXP_CLAMP = 80.0

MOE_SUB = 256
MOE_SLAB = 1536
MOE_TN1 = 512
MOE_TN2 = 512


def _cparams(sem):
    return pltpu.CompilerParams(dimension_semantics=sem, vmem_limit_bytes=VMEM_LIMIT)


def _ada_kernel(c_ref, w_ref, b_ref, o_ref):
    c = c_ref[...]
    s = (c * jax.nn.sigmoid(c)).astype(BF16)
    o_ref[...] = jnp.dot(s, w_ref[...].astype(BF16), preferred_element_type=F32) + b_ref[...]


def _ada_call(c_all, w_ada, b_ada, tn=1024):
    nb, d = c_all.shape
    depth, _, n = w_ada.shape
    return pl.pallas_call(
        _ada_kernel,
        grid=(depth, n // tn),
        in_specs=[
            pl.BlockSpec((nb, d), lambda l, j: (0, 0)),
            pl.BlockSpec((None, d, tn), lambda l, j: (l, 0, j)),
            pl.BlockSpec((None, 1, tn), lambda l, j: (l, 0, j)),
        ],
        out_specs=pl.BlockSpec((None, nb, tn), lambda l, j: (l, 0, j)),
        out_shape=jax.ShapeDtypeStruct((depth, nb, n), F32),
        compiler_params=_cparams(("arbitrary", "arbitrary")),
        name="ada_mod",
    )(c_all, w_ada, b_ada.reshape(depth, 1, n))


def _rms_mod(x, gn, sc, sh):
    ms = jnp.mean(x * x, axis=-1, keepdims=True)
    h = x * lax.rsqrt(ms + EPS) * gn
    return h * (1.0 + sc) + sh


_HI16 = 0xFFFF0000


def _pack_bf16_pairs(x, group):
    outs = []
    for c0 in range(0, x.shape[1], 2 * group):
        lo = lax.bitcast_convert_type(x[:, c0:c0 + group].astype(BF16).astype(F32), jnp.uint32) >> 16
        hi = lax.bitcast_convert_type(x[:, c0 + group:c0 + 2 * group].astype(BF16).astype(F32), jnp.uint32)
        outs.append((hi & jnp.uint32(_HI16)) | lo)
    return outs[0] if len(outs) == 1 else jnp.concatenate(outs, axis=1)


def _unpack_bf16_pairs(u, group):
    lo = lax.bitcast_convert_type(u << 16, F32)
    hi = lax.bitcast_convert_type(u & jnp.uint32(_HI16), F32)
    parts = []
    for c0 in range(0, u.shape[1], group):
        parts += [lo[:, c0:c0 + group], hi[:, c0:c0 + group]]
    return jnp.concatenate(parts, axis=1)


def _resid_norm_kernel(has_y, final, *refs):
    it = iter(refs)
    x_ref = next(it)
    if has_y:
        y_ref, gt_ref, g2_ref = next(it), next(it), next(it)
    gn_ref = next(it)
    if not final:
        sc_ref, sh_ref = next(it), next(it)
    if has_y and not final:
        xo_ref = next(it)
    o_ref = next(it)

    x = x_ref[...]
    if has_y:
        acc = None
        for k in range(TOP_K):
            t = gt_ref[:, k:k + 1] * _unpack_bf16_pairs(y_ref[k], MOE_TN2 // 2)
            acc = t if acc is None else acc + t
        x = x + g2_ref[...] * acc
        if not final:
            xo_ref[...] = x
    if final:
        ms = jnp.mean(x * x, axis=-1, keepdims=True)
        o_ref[...] = x * lax.rsqrt(ms + EPS) * gn_ref[...]
    else:
        o_ref[...] = _rms_mod(x, gn_ref[...], sc_ref[...], sh_ref[...]).astype(o_ref.dtype)


def _mod_spec(mod, tiles_per_group):
    _, r, d = mod.shape
    return pl.BlockSpec((None, r, d), lambda i, j=0: (i // tiles_per_group, 0, 0))


def _resid_norm_call(x, y4, gates, gate2, gnorm, scale, shift, *, tm, tiles_per_group, row_off, final):
    rows, d = x.shape
    has_y = y4 is not None
    row_spec = pl.BlockSpec((tm, d), lambda i: (i, 0))
    in_specs, args = [row_spec], [x]
    if has_y:
        off = row_off // tm
        in_specs += [pl.BlockSpec((TOP_K, tm, d // 2), lambda i: (0, i + off, 0)),
                     pl.BlockSpec((tm, TOP_K), lambda i: (i + off, 0)), _mod_spec(gate2, tiles_per_group)]
        args += [y4, gates, gate2]
    in_specs.append(pl.BlockSpec((1, d), lambda i: (0, 0)))
    args.append(gnorm)
    if not final:
        in_specs += [_mod_spec(scale, tiles_per_group), _mod_spec(shift, tiles_per_group)]
        args += [scale, shift]
    if final:
        out_specs, out_shape = row_spec, jax.ShapeDtypeStruct((rows, d), F32)
    elif has_y:
        out_specs = [row_spec, row_spec]
        out_shape = [jax.ShapeDtypeStruct((rows, d), F32), jax.ShapeDtypeStruct((rows, d), BF16)]
    else:
        out_specs, out_shape = row_spec, jax.ShapeDtypeStruct((rows, d), BF16)
    out = pl.pallas_call(
        functools.partial(_resid_norm_kernel, has_y, final),
        grid=(rows // tm,), in_specs=in_specs, out_specs=out_specs, out_shape=out_shape,
        compiler_params=_cparams(("arbitrary",)),
        name="resid_norm",
    )(*args)
    if final:
        return out
    return (out[0], out[1]) if has_y else (x, out)


def _matmul_kernel(x_ref, w_ref, o_ref):
    o_ref[...] = jnp.dot(x_ref[...], w_ref[...], preferred_element_type=F32)


def _inproj_call(h, w, layer, *, tm, tn):
    rows, d = h.shape
    n = w.shape[2]
    return pl.pallas_call(
        _matmul_kernel,
        grid=(rows // tm, n // tn),
        in_specs=[pl.BlockSpec((tm, d), lambda i, j: (i, 0)),
                  pl.BlockSpec((None, d, tn), lambda i, j: (layer, 0, j))],
        out_specs=pl.BlockSpec((tm, tn), lambda i, j: (i, j)),
        out_shape=jax.ShapeDtypeStruct((rows, n), F32),
        compiler_params=_cparams(("arbitrary", "arbitrary")),
        name="inproj",
    )(h, w)


def _head_masks(width=LANES):
    lane = lax.broadcasted_iota(jnp.int32, (1, width), 1) % LANES
    return lane < DK, lane >= DK


def _expand(x, bb, ls):
    if bb == 1:
        return x
    seq = lax.broadcasted_iota(jnp.int32, (bb * ls, 1), 0) // ls
    return jnp.concatenate([jnp.where(seq == b, x, 0.0) for b in range(bb)], axis=1)


def _dot_nt(a, b):
    return lax.dot_general(a, b, (((1,), (1,)), ((), ())), preferred_element_type=F32)


def _dot_tn(a, b, precision=None):
    return lax.dot_general(a, b, (((0,), (0,)), ((), ())), preferred_element_type=F32,
                           precision=precision)


def _silu(x):
    return x * jax.nn.sigmoid(x)


def _causal_mask(bb, ls):
    t = np.arange(bb * ls)
    same_seg = (t[:, None] // ls) == (t[None, :] // ls)
    return jnp.asarray((same_seg & (t[None, :] <= t[:, None])).astype(np.float32))


def _chain_states(s_in, dec, upd, bb):
    states = [s_in]
    for c in range(bb):
        blk = slice(c * LANES, (c + 1) * LANES)
        states.append(states[-1] * dec[blk] + upd[blk])
    s_all = states[0] if bb == 1 else jnp.concatenate(states[:bb], axis=0)
    return s_all, states[bb]


def _gla_body(q, k, v, g, ra, wg, bg, gout, cmask, s_in, bb, ls, chained):
    rows = bb * ls
    sub = min(GLA_SUB, ls)
    n_sub = ls // sub
    hm = _head_masks()
    seg = lax.broadcasted_iota(jnp.int32, (rows, 1), 0) // ls

    x = jnp.dot(ra.astype(BF16), wg, preferred_element_type=F32) + bg
    log_a = (jnp.minimum(x, 0.0) - jnp.log1p(jnp.exp(-jnp.abs(x)))) * (1.0 / GLA_TAU)
    la_hi = log_a.astype(BF16)
    la_r = log_a - la_hi.astype(F32)
    la_mid = la_r.astype(BF16)
    la_lo = (la_r - la_mid.astype(F32)).astype(BF16)
    b3 = jnp.dot(cmask.astype(BF16), jnp.concatenate([la_hi, la_mid, la_lo], axis=1),
                 preferred_element_type=F32)
    bcum = b3[:, :LANES] + b3[:, LANES:2 * LANES] + b3[:, 2 * LANES:]

    def seg_row(offset):
        out = bcum[offset:offset + 1, :]
        for b in range(1, bb):
            out = jnp.where(seg == b, bcum[b * ls + offset:b * ls + offset + 1, :], out)
        return out

    bend = seg_row(ls - 1)
    bases = [None] + [seg_row(i * sub - 1) for i in range(1, n_sub)]
    sub_id = (lax.broadcasted_iota(jnp.int32, (rows, 1), 0) % ls) // sub
    bs = bcum
    for i in range(1, n_sub):
        bs = jnp.where(sub_id == i, bcum - bases[i], bs)

    qs = q * (DK ** -0.5)
    q_sub = qs * jnp.exp(bs)
    q_seq = qs * jnp.exp(bcum)
    k_end = k * jnp.exp(bend - bcum)

    a_parts, k_parts = [], []
    for i in range(n_sub):
        shift = -bcum if i == 0 else bases[i] - bcum
        k_parts.append((k * jnp.exp(jnp.minimum(shift, EXP_CLAMP))).astype(BF16))
        a_parts.append(jnp.where(sub_id == i, q_sub, 0.0) if n_sub > 1 else q_sub)
    a = a_parts[0] if n_sub == 1 else jnp.concatenate(a_parts, axis=1)
    kk = k_parts[0] if n_sub == 1 else jnp.concatenate(k_parts, axis=1)
    hmw = _head_masks(n_sub * LANES)
    a2 = jnp.concatenate([jnp.where(hmw[0], a, 0.0), jnp.where(hmw[1], a, 0.0)], axis=0).astype(BF16)
    att = _dot_nt(a2, kk)
    att = jnp.where(jnp.concatenate([cmask, cmask], axis=0) > 0.0, att, 0.0)

    vhs = [v[:, h * DV:(h + 1) * DV].astype(BF16) for h in range(2)]
    upd = (_dot_tn(_expand(jnp.where(hm[0], k_end, 0.0), bb, ls).astype(BF16), vhs[0])
           + _dot_tn(_expand(jnp.where(hm[1], k_end, 0.0), bb, ls).astype(BF16), vhs[1]))
    dec_parts = []
    for b in range(bb):
        last = b * ls + ls - 1
        dec_parts.append(jnp.exp(jnp.broadcast_to(bcum[last:last + 1, :], (LANES, LANES)).T))
    dec = dec_parts[0] if bb == 1 else jnp.concatenate(dec_parts, axis=0)
    if chained:
        s_all, s_out = _chain_states(s_in, dec, upd, bb)
    else:
        s_all, s_out = s_in, s_in * dec + upd

    s_bf = s_all.astype(BF16)
    outs = []
    for h in range(2):
        o = jnp.dot(att[h * rows:(h + 1) * rows].astype(BF16), vhs[h], preferred_element_type=F32)
        o = o + jnp.dot(_expand(jnp.where(hm[h], q_seq, 0.0), bb, ls).astype(BF16), s_bf,
                        preferred_element_type=F32)
        o = o * lax.rsqrt(jnp.mean(o * o, axis=-1, keepdims=True) + EPS) * gout
        outs.append(o * _silu(g[:, h * DV:(h + 1) * DV]))
    return jnp.concatenate(outs, axis=1), s_out


def _gla_kernel(cfg, *refs):
    bb, ls, chained, has_s0 = cfg
    (q_ref, k_ref, v_ref, g_ref, ra_ref, wg_ref, bg_ref, gout_ref, cm_ref) = refs[:9]
    if has_s0:
        s0_ref, o_ref, s_ref = refs[9:]
    else:
        o_ref, s_ref = refs[9:]

    @pl.when(pl.program_id(2) == 0)
    def _():
        s_ref[...] = s0_ref[...] if has_s0 else jnp.zeros(s_ref.shape, F32)

    s = s_ref[...].reshape(s_ref.shape[0] * LANES, LANES)
    o, s = _gla_body(q_ref[...], k_ref[...], v_ref[...], g_ref[...], ra_ref[...], wg_ref[...], bg_ref[...],
                     gout_ref[...], cm_ref[...], s, bb, ls, chained)
    o_ref[...] = o.astype(o_ref.dtype)
    s_ref[...] = s.reshape(s_ref.shape)


def _mixer_grid(n_seq, t, bb, ls, chained):
    tl = bb * ls
    if chained:
        steps = t // tl
        return tl, (n_seq, N_PAIR, steps), (lambda o, j, c: o * steps + c), 1
    assert t == ls
    return tl, (n_seq // bb, N_PAIR, 1), (lambda o, j, c: o), bb


def _gla_call(p, wg, bg, gout, s0, *, n_seq, t, bb, ls, chained):
    rows_total = n_seq * t
    tl, grid, row, sb = _mixer_grid(n_seq, t, bb, ls, chained)
    has_s0 = s0 is not None
    cm = _causal_mask(bb, ls)
    r = tl

    def col(width, base):
        return pl.BlockSpec((tl, width), lambda o, j, c: (row(o, j, c), base // width + j))

    const = lambda shape: pl.BlockSpec(shape, lambda o, j, c: (0,) * len(shape))
    st_spec = pl.BlockSpec((sb, None, LANES, LANES), lambda o, j, c: (o, j, 0, 0))
    in_specs = [col(LANES, COL_QA), col(LANES, COL_KA), col(2 * DV, COL_VA), col(2 * DV, COL_GA),
                pl.BlockSpec((tl, LANES), lambda o, j, c: (row(o, j, c), COL_RA // LANES)),
                pl.BlockSpec((LANES, LANES), lambda o, j, c: (0, j)),
                pl.BlockSpec((1, LANES), lambda o, j, c: (0, j)),
                const((1, DV)), const((r, r))]
    args = [p, p, p, p, p, wg, bg, gout, cm]
    if has_s0:
        in_specs.append(st_spec)
        args.append(s0)
    return pl.pallas_call(
        functools.partial(_gla_kernel, (bb, ls, chained, has_s0)),
        grid=grid, in_specs=in_specs,
        out_specs=[pl.BlockSpec((tl, 2 * DV), lambda o, j, c: (row(o, j, c), j)), st_spec],
        out_shape=[jax.ShapeDtypeStruct((rows_total, N_HEADS * DV), BF16),
                   jax.ShapeDtypeStruct((n_seq, N_PAIR, LANES, LANES), F32)],
        compiler_params=_cparams(("arbitrary", "arbitrary", "arbitrary")),
        name="gla_mixer",
    )(*args)


def _ret_tables(bb, ls, pos0):
    rows = bb * ls
    log_g = np.log1p(-np.exp2(-5.0 - np.arange(N_HEADS, dtype=np.float64)))
    t = np.arange(rows)
    n = t % ls
    same_seq = (t[:, None] // ls) == (t[None, :] // ls)
    diff = (n[:, None] - n[None, :]).astype(np.float64)
    causal = same_seq & (diff >= 0)
    dmask = np.where(causal[None], np.exp(np.maximum(diff, 0.0)[None] * log_g[:, None, None]), 0.0)
    lane_head = np.arange(N_PAIR * LANES) // DK
    q_dec = np.exp((n[:, None] + 1.0) * log_g[lane_head][None])
    k_dec = np.exp((ls - 1.0 - n)[:, None] * log_g[lane_head][None])
    c_dec = np.exp(ls * log_g[lane_head])[:, None] * np.ones((1, LANES))
    c_dec = np.tile(c_dec.reshape(N_PAIR, 1, LANES, LANES), (1, bb, 1, 1)).reshape(N_PAIR, bb * LANES, LANES)
    f32 = lambda a: jnp.asarray(a, dtype=F32)
    return f32(dmask), f32(q_dec), f32(k_dec), f32(c_dec)


def _rope_tables(t, pos0):
    half = DK // 2
    inv = ROPE_BASE ** (-np.arange(half, dtype=np.float64) / half)
    ang = (np.arange(t, dtype=np.float64) + pos0)[:, None] * inv[None]
    cos = np.tile(np.cos(ang), (1, 2 * LANES // DK))
    sin = np.tile(np.concatenate([-np.sin(ang), np.sin(ang)], axis=1), (1, LANES // DK))
    return jnp.asarray(cos, dtype=F32), jnp.asarray(sin, dtype=F32)


def _rope(x, cos, sin_signed):
    lane = lax.broadcasted_iota(jnp.int32, (1, LANES), 1) % DK
    swapped = jnp.where(lane < DK // 2, pltpu.roll(x, LANES - DK // 2, 1), pltpu.roll(x, DK // 2, 1))
    return x * cos + swapped * sin_signed


def _ret_body(q, k, v, g, cos, sin, dmask, q_dec, k_dec, c_dec, gn, s_in, bb, ls, chained):
    hm = _head_masks()
    qr = _rope(q, cos, sin)
    kr = _rope(k, cos, sin) * (DK ** -0.5)
    kr_bf = kr.astype(BF16)
    q_in = qr * q_dec
    k_end = kr * k_dec
    vhs = [v[:, h * DV:(h + 1) * DV].astype(BF16) for h in range(2)]
    upd = (_dot_tn(_expand(jnp.where(hm[0], k_end, 0.0), bb, ls).astype(BF16), vhs[0])
           + _dot_tn(_expand(jnp.where(hm[1], k_end, 0.0), bb, ls).astype(BF16), vhs[1]))
    if chained:
        s_all, s_out = _chain_states(s_in, c_dec, upd, bb)
    else:
        s_all, s_out = s_in, s_in * c_dec + upd
    s_bf = s_all.astype(BF16)
    outs = []
    for h in range(2):
        att = _dot_nt(jnp.where(hm[h], qr, 0.0).astype(BF16), kr_bf) * dmask[h]
        o = jnp.dot(att.astype(BF16), vhs[h], preferred_element_type=F32)
        o = o + jnp.dot(_expand(jnp.where(hm[h], q_in, 0.0), bb, ls).astype(BF16), s_bf,
                        preferred_element_type=F32)
        d = o - jnp.mean(o, axis=-1, keepdims=True)
        var = jnp.mean(d * d, axis=-1, keepdims=True)
        o = d * lax.rsqrt(var + EPS) * gn[:, h * DV:(h + 1) * DV]
        outs.append(o * _silu(g[:, h * DV:(h + 1) * DV]))
    return jnp.concatenate(outs, axis=1), s_out


def _ret_kernel(cfg, *refs):
    bb, ls, chained, has_s0 = cfg
    (q_ref, k_ref, v_ref, g_ref, cos_ref, sin_ref, dm_ref, qd_ref, kd_ref, cd_ref, gn_ref) = refs[:11]
    if has_s0:
        s0_ref, o_ref, s_ref = refs[11:]
    else:
        o_ref, s_ref = refs[11:]

    @pl.when(pl.program_id(2) == 0)
    def _():
        s_ref[...] = s0_ref[...] if has_s0 else jnp.zeros(s_ref.shape, F32)

    s = s_ref[...].reshape(s_ref.shape[0] * LANES, LANES)
    o, s = _ret_body(q_ref[...], k_ref[...], v_ref[...], g_ref[...], cos_ref[...], sin_ref[...], dm_ref[...],
                     qd_ref[...], kd_ref[...], cd_ref[...], gn_ref[...], s, bb, ls, chained)
    o_ref[...] = o.astype(o_ref.dtype)
    s_ref[...] = s.reshape(s_ref.shape)


def _ret_call(p, gn, s0, *, n_seq, t, bb, ls, chained, pos0):
    rows_total = n_seq * t
    tl, grid, row, sb = _mixer_grid(n_seq, t, bb, ls, chained)
    has_s0 = s0 is not None
    dmask, q_dec, k_dec, c_dec = _ret_tables(bb, ls, pos0)
    cos, sin = _rope_tables(t, pos0)
    r = tl
    if chained:
        rope_spec = pl.BlockSpec((tl, LANES), lambda o, j, c: (c, 0))
    else:
        cos, sin = jnp.tile(cos, (bb, 1)), jnp.tile(sin, (bb, 1))
        rope_spec = pl.BlockSpec((tl, LANES), lambda o, j, c: (0, 0))

    def col(width, base):
        return pl.BlockSpec((tl, width), lambda o, j, c: (row(o, j, c), base // width + j))

    st_spec = pl.BlockSpec((sb, None, LANES, LANES), lambda o, j, c: (o, j, 0, 0))
    in_specs = [col(LANES, COL_QC), col(LANES, COL_KC), col(2 * DV, COL_VC), col(2 * DV, COL_GC),
                rope_spec, rope_spec,
                pl.BlockSpec((2, r, r), lambda o, j, c: (j, 0, 0)),
                pl.BlockSpec((r, LANES), lambda o, j, c: (0, j)),
                pl.BlockSpec((r, LANES), lambda o, j, c: (0, j)),
                pl.BlockSpec((None, bb * LANES, LANES), lambda o, j, c: (j, 0, 0)),
                pl.BlockSpec((1, 2 * DV), lambda o, j, c: (0, j))]
    args = [p, p, p, p, cos, sin, dmask, q_dec, k_dec, c_dec, gn]
    if has_s0:
        in_specs.append(st_spec)
        args.append(s0)
    return pl.pallas_call(
        functools.partial(_ret_kernel, (bb, ls, chained, has_s0)),
        grid=grid, in_specs=in_specs,
        out_specs=[pl.BlockSpec((tl, 2 * DV), lambda o, j, c: (row(o, j, c), j)), st_spec],
        out_shape=[jax.ShapeDtypeStruct((rows_total, N_HEADS * DV), BF16),
                   jax.ShapeDtypeStruct((n_seq, N_PAIR, LANES, LANES), F32)],
        compiler_params=_cparams(("arbitrary", "arbitrary", "arbitrary")),
        name="ret_mixer",
    )(*args)


def _gelu(x):
    return 0.5 * x * (1.0 + jnp.tanh(0.7978845608028654 * (x + 0.044715 * x * x * x)))


def _cmlp_kernel(n_tiles, want_v, u_ref, v_ref, w_ref, bias_ref, lg_ref, lb_ref, *out_refs):
    o_ref = out_refs[0]
    for ti in range(n_tiles):
        sl = slice(ti * CHUNK_B, (ti + 1) * CHUNK_B)
        u = _gelu(u_ref[sl, :])
        vv = _gelu(v_ref[sl, :])
        vn_parts, z_parts = [], []
        for gi in range(G_B):
            vg = vv[:, gi * CG_B:(gi + 1) * CG_B]
            d = vg - jnp.mean(vg, axis=-1, keepdims=True)
            vn = d * lax.rsqrt(jnp.mean(d * d, axis=-1, keepdims=True) + EPS)
            vn = vn * lg_ref[:, gi * CG_B:(gi + 1) * CG_B] + lb_ref[:, gi * CG_B:(gi + 1) * CG_B]
            vn_parts.append(vn)
            z_parts.append(jnp.dot(w_ref[gi], vn.astype(BF16), preferred_element_type=F32))
        z = jnp.concatenate(z_parts, axis=1) + bias_ref[...]
        o_ref[sl, :] = (u * z).astype(o_ref.dtype)
        if want_v:
            out_refs[1][sl, :] = jnp.concatenate(vn_parts, axis=1)


def _cmlp_call(p, w_mix, bias_tile, ln_g, ln_b, *, n_tiles, want_v):
    rows = p.shape[0]
    tl = n_tiles * CHUNK_B
    const = lambda shape: pl.BlockSpec(shape, lambda i: (0,) * len(shape))
    out_specs = [pl.BlockSpec((tl, W_B), lambda i: (i, 0))]
    out_shape = [jax.ShapeDtypeStruct((rows, W_B), BF16)]
    if want_v:
        out_specs.append(pl.BlockSpec((tl, W_B), lambda i: (i, 0)))
        out_shape.append(jax.ShapeDtypeStruct((rows, W_B), F32))
    out = pl.pallas_call(
        functools.partial(_cmlp_kernel, n_tiles, want_v),
        grid=(rows // tl,),
        in_specs=[pl.BlockSpec((tl, W_B), lambda i: (i, COL_UB // W_B)),
                  pl.BlockSpec((tl, W_B), lambda i: (i, COL_VB // W_B)),
                  const((G_B, CHUNK_B, CHUNK_B)), const((CHUNK_B, W_B)), const((1, W_B)), const((1, W_B))],
        out_specs=out_specs, out_shape=out_shape,
        compiler_params=_cparams(("arbitrary",)),
        name="cmlp_mixer",
    )(p, p, w_mix, bias_tile, ln_g, ln_b)
    return (out[0], out[1]) if want_v else (out[0], None)


def _outproj_kernel(oa_ref, ob_ref, oc_ref, w_ref, x_ref, g1_ref, gn_ref, sc_ref, sh_ref,
                    wr_ref, br_ref, xo_ref, h_ref, lg_ref):
    wa = N_HEADS * DV
    acc = jnp.dot(oa_ref[...], w_ref[0:wa, :], preferred_element_type=F32)
    acc = acc + jnp.dot(ob_ref[...], w_ref[wa:wa + W_B, :], preferred_element_type=F32)
    acc = acc + jnp.dot(oc_ref[...], w_ref[wa + W_B:, :], preferred_element_type=F32)
    x = x_ref[...] + g1_ref[...] * acc
    xo_ref[...] = x
    h = _rms_mod(x, gn_ref[...], sc_ref[...], sh_ref[...])
    h_ref[...] = _pack_bf16_pairs(h, h.shape[1] // 2)
    hi = h.astype(BF16)
    lo = (h - hi.astype(F32)).astype(BF16)
    r1 = jnp.dot(hi, wr_ref[...], preferred_element_type=F32)
    r2 = jnp.dot(lo, wr_ref[:, :LANES], preferred_element_type=F32)
    lg_ref[...] = r1[:, :LANES] + r1[:, LANES:] + r2 + br_ref[...]


def _outproj_call(oa, ob, oc, w_out, layer, x, gate1, gnorm, scale, shift, w_router, b_router, *, tm,
                  tiles_per_group):
    rows, d = x.shape
    rspec = lambda width: pl.BlockSpec((tm, width), lambda i: (i, 0))
    const = lambda shape: pl.BlockSpec(shape, lambda i: (0,) * len(shape))
    ms = lambda m: _mod_spec(m, tiles_per_group)
    return pl.pallas_call(
        _outproj_kernel,
        grid=(rows // tm,),
        in_specs=[rspec(oa.shape[1]), rspec(ob.shape[1]), rspec(oc.shape[1]),
                  pl.BlockSpec((None,) + w_out.shape[1:], lambda i: (layer, 0, 0)),
                  rspec(d), ms(gate1), const((1, d)), ms(scale), ms(shift),
                  const(w_router.shape), const(b_router.shape)],
        out_specs=[rspec(d), rspec(d // 2), rspec(LANES)],
        out_shape=[jax.ShapeDtypeStruct((rows, d), F32), jax.ShapeDtypeStruct((rows, d // 2), jnp.uint32),
                   jax.ShapeDtypeStruct((rows, LANES), F32)],
        compiler_params=_cparams(("arbitrary",)),
        name="outproj_norm_router",
    )(oa, ob, oc, w_out, x, gate1, gnorm, scale, shift, w_router, b_router)


def _moe_in_kernel(se_ref, nb_ref, so_ref, x_ref, wg_ref, wu_ref, bg_ref, bu_ref, h_ref, x_s, wg_s, wu_s):
    nb = nb_ref[pl.program_id(0)]

    @pl.when(pl.program_id(1) == 0)
    def _():
        for rb in range(MOE_SLAB // MOE_SUB):
            sl = slice(rb * MOE_SUB, (rb + 1) * MOE_SUB)

            @pl.when(rb < nb)
            def _():
                x_s[sl, :] = _unpack_bf16_pairs(x_ref[sl, :], x_ref.shape[1]).astype(BF16)

    wg_s[...] = wg_ref[...].astype(BF16)
    wu_s[...] = wu_ref[...].astype(BF16)
    for n in range(1, MOE_SLAB // MOE_SUB + 1):
        rows = n * MOE_SUB

        @pl.when(nb == n)
        def _():
            for rb in range(n):
                sl = slice(rb * MOE_SUB, (rb + 1) * MOE_SUB)
                xb = x_s[sl, :]
                gt = jnp.dot(xb, wg_s[...], preferred_element_type=F32) + bg_ref[...]
                up = jnp.dot(xb, wu_s[...], preferred_element_type=F32) + bu_ref[...]
                gt = jnp.minimum(gt, SWIGLU_LIMIT)
                up = jnp.clip(up, -SWIGLU_LIMIT, SWIGLU_LIMIT)
                glu = gt * jax.nn.sigmoid(SWIGLU_ALPHA * gt)
                h_ref[sl, :] = ((up + 1.0) * glu).astype(h_ref.dtype)
            if rows < MOE_SLAB:
                h_ref[rows:, :] = jnp.zeros((MOE_SLAB - rows, h_ref.shape[1]), h_ref.dtype)


def _moe_out_kernel(se_ref, nb_ref, so_ref, h_ref, w_ref, b_ref, y_ref, w_s):
    nb = nb_ref[pl.program_id(0)]
    w_s[...] = w_ref[...].astype(BF16)
    for n in range(1, MOE_SLAB // MOE_SUB + 1):
        rows = n * MOE_SUB

        @pl.when(nb == n)
        def _():
            y = jnp.dot(h_ref[0:rows, :], w_s[...], preferred_element_type=F32) + b_ref[...]
            y_ref[0:rows, :] = _pack_bf16_pairs(y, MOE_TN2 // 2)
            if rows < MOE_SLAB:
                y_ref[rows:, :] = jnp.zeros((MOE_SLAB - rows, y_ref.shape[1]), y_ref.dtype)


def _moe_calls(x_rows, n_used, slab_e, slab_nb, slab_off, w_in, b_in, w_out, b_out, layer):
    n_slab = slab_e.shape[0]
    r = MOE_SLAB
    dh = x_rows.shape[1]
    d = 2 * dh
    nt1 = D_FF // MOE_TN1
    nt2 = d // MOE_TN2
    up_off = D_FF // MOE_TN1

    def wspec(k, tn, off=0):
        return pl.BlockSpec((None, None, k, tn), lambda s, j, se, nb, so: (layer, se[s], 0, off + j))

    h = pl.pallas_call(
        _moe_in_kernel,
        grid_spec=pltpu.PrefetchScalarGridSpec(
            num_scalar_prefetch=3, grid=(n_used, nt1),
            in_specs=[
                pl.BlockSpec((pl.Element(r), pl.Element(dh)), lambda s, j, se, nb, so: (so[s] * MOE_SUB, 0)),
                wspec(d, MOE_TN1), wspec(d, MOE_TN1, up_off), wspec(1, MOE_TN1), wspec(1, MOE_TN1, up_off),
            ],
            out_specs=pl.BlockSpec((None, r, MOE_TN1), lambda s, j, se, nb, so: (s, 0, j)),
            scratch_shapes=[pltpu.VMEM((r, d), BF16), pltpu.VMEM((d, MOE_TN1), BF16),
                            pltpu.VMEM((d, MOE_TN1), BF16)]),
        out_shape=jax.ShapeDtypeStruct((n_slab, r, D_FF), BF16),
        compiler_params=_cparams(("arbitrary", "arbitrary")),
        name="moe_glu",
    )(slab_e, slab_nb, slab_off, x_rows, w_in, w_in, b_in, b_in)

    y = pl.pallas_call(
        _moe_out_kernel,
        grid_spec=pltpu.PrefetchScalarGridSpec(
            num_scalar_prefetch=3, grid=(n_used, nt2),
            in_specs=[
                pl.BlockSpec((None, r, D_FF), lambda s, j, se, nb, so: (s, 0, 0)),
                wspec(D_FF, MOE_TN2), wspec(1, MOE_TN2),
            ],
            out_specs=pl.BlockSpec((None, r, MOE_TN2 // 2), lambda s, j, se, nb, so: (s, 0, j)),
            scratch_shapes=[pltpu.VMEM((D_FF, MOE_TN2), BF16)]),
        out_shape=jax.ShapeDtypeStruct((n_slab, r, dh), jnp.uint32),
        compiler_params=_cparams(("arbitrary", "arbitrary")),
        name="moe_down",
    )(slab_e, slab_nb, slab_off, h, w_out, b_out)
    return y


def _rows(a, idx):
    return a.at[idx].get(mode="promise_in_bounds")


def _count_le(bounds, x):
    return jnp.sum((bounds <= x[..., None]).astype(jnp.int32), axis=-1)


def _n_slabs(n_assign):
    return N_EXPERTS + (n_assign + N_EXPERTS * (MOE_SUB - 1)) // MOE_SLAB


def _route(logits, n_slab):
    m = logits.shape[0]
    top_val, top_idx = lax.top_k(logits, TOP_K)
    gates = jax.nn.softmax(top_val, axis=-1)
    a = m * TOP_K
    flat_e = top_idx.reshape(a).astype(jnp.int32)
    ar = jnp.arange(a, dtype=jnp.int32)
    sorted_e, order = lax.sort((flat_e, ar), num_keys=1, is_stable=True)
    counts = jnp.bincount(flat_e, length=N_EXPERTS).astype(jnp.int32)
    start = jnp.cumsum(counts) - counts
    rank = ar - start[sorted_e]
    padded = (counts + MOE_SUB - 1) // MOE_SUB * MOE_SUB
    nslab_e = (padded + MOE_SLAB - 1) // MOE_SLAB
    slab_end = jnp.cumsum(nslab_e)
    slab_start = slab_end - nslab_e
    dest = (slab_start[sorted_e] + rank // MOE_SLAB) * MOE_SLAB + rank % MOE_SLAB
    _, pos = lax.sort((order, dest), num_keys=1)
    pos = pos.reshape(m, TOP_K)
    n_used = slab_end[-1]
    sid = jnp.arange(n_slab, dtype=jnp.int32)
    slab_e = jnp.minimum(_count_le(slab_end, sid), N_EXPERTS - 1)
    local = sid - slab_start[slab_e]
    slab_nb = (jnp.clip(padded[slab_e] - local * MOE_SLAB, 0, MOE_SLAB) // MOE_SUB).astype(jnp.int32)
    cend = jnp.cumsum(padded)
    cstart = cend - padded
    slab_off = ((cstart[slab_e] + local * MOE_SLAB) // MOE_SUB).astype(jnp.int32)
    n_rows = a + N_EXPERTS * MOE_SUB + MOE_SLAB
    rid = jnp.arange(n_rows, dtype=jnp.int32)
    e_r = jnp.minimum(_count_le(cend, rid), N_EXPERTS - 1)
    rk = rid - cstart[e_r]
    src = jnp.clip(start[e_r] + rk, 0, a - 1)
    row_tok = jnp.where(rk < counts[e_r], _rows(order, src) // TOP_K, rid % m)
    return gates, pos, row_tok, n_used.astype(jnp.int32), slab_e, slab_nb, slab_off


def _moe(h2u, logits, w_in, b_in, w_out, b_out, layer):
    m, dh = h2u.shape
    n_slab = _n_slabs(m * TOP_K)
    gates, pos, row_tok, n_used, slab_e, slab_nb, slab_off = _route(logits, n_slab)
    x_rows = _rows(h2u, row_tok)
    y = _moe_calls(x_rows, n_used, slab_e, slab_nb, slab_off, w_in, b_in.reshape(b_in.shape[0], N_EXPERTS, 1, -1),
                   w_out, b_out.reshape(b_out.shape[0], N_EXPERTS, 1, -1), layer)
    y4 = _rows(y.reshape(n_slab * MOE_SLAB, dh), pos.T.reshape(-1))
    return y4.reshape(TOP_K, m, dh), gates


def _prep_w_in(w_in):
    depth, d, _ = w_in.shape
    w = w_in.astype(BF16)
    pad = jnp.zeros((depth, d, D_P - COL_RA - GLA_LOWRANK), BF16)
    return jnp.concatenate([w[:, :, _O_UB:_O_QC], w[:, :, :_O_RA], w[:, :, _O_QC:_O_END],
                            w[:, :, _O_RA:_O_UB], pad], axis=-1)


def _prep_cmlp(w_spatial, b_spatial, t):
    tri = jnp.tril(jnp.ones((CHUNK_B, CHUNK_B), bool))
    w = jnp.where(tri[None, None], w_spatial, 0.0)
    if t < CHUNK_B:
        reps = CHUNK_B // t
        eye = jnp.eye(reps, dtype=w.dtype)
        w = jnp.einsum("ab,lgij->lgaibj", eye, w[:, :, :t, :t]).reshape(w.shape[0], G_B, CHUNK_B, CHUNK_B)
        b = jnp.tile(b_spatial[:, :, :t], (1, 1, reps))
    else:
        b = b_spatial
    bias = jnp.repeat(jnp.swapaxes(b, 1, 2), CG_B, axis=2)
    return w.astype(BF16), bias.astype(F32)


def kernel(x_prompt, x_sample, c_prompt, c_sample, state_gla, state_ret, w_ada, b_ada, g_norm1,
           g_norm2, w_in, w_gla_gate, b_gla_gate, g_gla_out, ln_g_cmlp, ln_b_cmlp, w_spatial,
           b_spatial, g_ret_out, w_out, w_router, b_router, w_e_in, b_e_in, w_e_out, b_e_out, g_final):
    bp, tp, d = x_prompt.shape
    bs, ts, _ = x_sample.shape
    rows_p, rows_s = bp * tp, bs * ts
    tmn_p, tmn_s = 256, 256
    tmo_p, tmo_s = 512, 256
    tm_in, tn_in = 1024, 1152
    dec_bb = LANES // ts

    n_c = bp + bs
    n_c_pad = -(-n_c // SUBLANES) * SUBLANES
    c_all = jnp.concatenate([c_prompt, c_sample, jnp.zeros((n_c_pad - n_c, d), F32)], axis=0)
    mod = _ada_call(c_all, w_ada, b_ada).reshape(DEPTH, n_c_pad, N_MOD, d)
    mod_p = jnp.transpose(mod[:, :bp], (0, 2, 1, 3)).reshape(DEPTH, N_MOD, bp, 1, d)
    mod_s = jnp.repeat(jnp.transpose(mod[:, bp:n_c], (0, 2, 1, 3)), ts, axis=2)

    w_in_r = _prep_w_in(w_in)
    w_out_bf = w_out.astype(BF16)
    wg_pad = jnp.pad(w_gla_gate, ((0, 0), (0, LANES - GLA_LOWRANK), (0, 0))).astype(BF16)
    w_router_pad = jnp.pad(w_router, ((0, 0), (0, 0), (0, LANES - N_EXPERTS)))
    w_router_hi = w_router_pad.astype(BF16)
    w_router_lo = (w_router_pad - w_router_hi.astype(F32)).astype(BF16)
    w_router_2 = jnp.concatenate([w_router_hi, w_router_lo], axis=-1)
    b_router_pad = jnp.pad(b_router, ((0, 0), (0, LANES - N_EXPERTS))).reshape(DEPTH, 1, LANES)
    wmix_p, bias_p = _prep_cmlp(w_spatial, b_spatial, CHUNK_B)
    wmix_s, bias_s = _prep_cmlp(w_spatial, b_spatial, ts)

    xp = x_prompt.reshape(rows_p, d)
    xs = x_sample.reshape(rows_s, d)
    st_gla = state_gla.reshape(DEPTH, bs, N_PAIR, LANES, LANES)
    st_ret = state_ret.reshape(DEPTH, bs, N_PAIR, LANES, LANES)
    y4 = gates = None
    outs = {k: [] for k in ("gla_p", "ret_p", "gla_s", "ret_s", "v_s")}

    for l in range(DEPTH):
        row = lambda a: a[l].reshape(1, -1)
        mp = lambda i: mod_p[l, i]
        msf = lambda i, tm: mod_s[l, i].reshape(rows_s // tm, tm, d)
        g2p = mod_p[l - 1, 5] if l > 0 else None
        g2s = mod_s[l - 1, 5].reshape(rows_s // tmn_s, tmn_s, d) if l > 0 else None

        xp, hp = _resid_norm_call(xp, y4, gates, g2p, row(g_norm1), mp(1), mp(0), tm=tmn_p,
                                  tiles_per_group=tp // tmn_p, row_off=0, final=False)
        xs, hs = _resid_norm_call(xs, y4, gates, g2s, row(g_norm1), msf(1, tmn_s), msf(0, tmn_s), tm=tmn_s,
                                  tiles_per_group=1, row_off=rows_p, final=False)
        pp = _inproj_call(hp, w_in_r, l, tm=tm_in, tn=tn_in)
        ps = _inproj_call(hs, w_in_r, l, tm=tm_in, tn=tn_in)

        bg = row(b_gla_gate)
        gout = row(g_gla_out)
        gret = row(g_ret_out)
        oa_p, sa_p = _gla_call(pp, wg_pad[l], bg, gout, None, n_seq=bp, t=tp, bb=4, ls=GLA_CHUNK, chained=True)
        oa_s, sa_s = _gla_call(ps, wg_pad[l], bg, gout, st_gla[l], n_seq=bs, t=ts, bb=dec_bb, ls=ts,
                               chained=False)
        oc_p, sc_p = _ret_call(pp, gret, None, n_seq=bp, t=tp, bb=2, ls=RET_CHUNK, chained=True, pos0=0)
        oc_s, sc_s = _ret_call(ps, gret, st_ret[l], n_seq=bs, t=ts, bb=dec_bb, ls=ts, chained=False,
                               pos0=PAST_LEN)
        ob_p, _ = _cmlp_call(pp, wmix_p[l], bias_p[l], row(ln_g_cmlp), row(ln_b_cmlp), n_tiles=4, want_v=False)
        ob_s, v_s = _cmlp_call(ps, wmix_s[l], bias_s[l], row(ln_g_cmlp), row(ln_b_cmlp), n_tiles=4, want_v=True)

        xp, h2p, lgp = _outproj_call(oa_p, ob_p, oc_p, w_out_bf, l, xp, mp(2), row(g_norm2), mp(4), mp(3),
                                     w_router_2[l], b_router_pad[l], tm=tmo_p, tiles_per_group=tp // tmo_p)
        xs, h2s, lgs = _outproj_call(oa_s, ob_s, oc_s, w_out_bf, l, xs, msf(2, tmo_s), row(g_norm2),
                                     msf(4, tmo_s), msf(3, tmo_s), w_router_2[l], b_router_pad[l],
                                     tm=tmo_s, tiles_per_group=1)

        h2 = jnp.concatenate([h2p, h2s], axis=0)
        logits = jnp.concatenate([lgp, lgs], axis=0)[:, :N_EXPERTS]
        y4, gates = _moe(h2, logits, w_e_in, b_e_in, w_e_out, b_e_out, l)

        outs["gla_p"].append(sa_p.reshape(bp, N_HEADS, DK, DV))
        outs["ret_p"].append(sc_p.reshape(bp, N_HEADS, DK, DV))
        outs["gla_s"].append(sa_s.reshape(bs, N_HEADS, DK, DV))
        outs["ret_s"].append(sc_s.reshape(bs, N_HEADS, DK, DV))
        outs["v_s"].append(v_s.reshape(bs, ts, W_B))

    yp = _resid_norm_call(xp, y4, gates, mod_p[DEPTH - 1, 5], g_final.reshape(1, d), None, None, tm=tmn_p,
                          tiles_per_group=tp // tmn_p, row_off=0, final=True)
    ys = _resid_norm_call(xs, y4, gates, mod_s[DEPTH - 1, 5].reshape(rows_s // tmn_s, tmn_s, d),
                          g_final.reshape(1, d), None, None, tm=tmn_s, tiles_per_group=1, row_off=rows_p,
                          final=True)
    return (yp.reshape(bp, tp, d), ys.reshape(bs, ts, d), jnp.stack(outs["gla_p"]), jnp.stack(outs["ret_p"]),
            jnp.stack(outs["gla_s"]), jnp.stack(outs["ret_s"]), jnp.stack(outs["v_s"]))
```

```python
import functools

import numpy as np
import jax
import jax.numpy as jnp
from jax import lax
from jax.experimental import pallas as pl
from jax.experimental.pallas import tpu as pltpu

F32 = jnp.float32
BF16 = jnp.bfloat16
HIGHEST = lax.Precision.HIGHEST

D_MODEL = 2048
DEPTH = 4
N_HEADS = 6
N_PAIR = N_HEADS // 2
DK = 64
DV = 128
GLA_LOWRANK = 16
GLA_TAU = 16.0
GLA_CHUNK = 64
GLA_SUB = 16
G_B = 4
CG_B = 128
W_B = G_B * CG_B
CHUNK_B = 128
RET_CHUNK = 128
ROPE_BASE = 10000.0
N_EXPERTS = 32
TOP_K = 4
D_FF = D_MODEL
SWIGLU_LIMIT = 7.0
SWIGLU_ALPHA = 1.702
N_MOD = 6
EPS = 1e-6
PAST_LEN = 16384

LANES = 128
SUBLANES = 8
VMEM_LIMIT = 56 * 1024 * 1024

COL_UB = 0
COL_VB = 512
COL_QA = 1024
COL_KA = 1408
COL_VA = 1792
COL_GA = 2560
COL_QC = 3328
COL_KC = 3712
COL_VC = 4096
COL_GC = 4864
COL_RA = 5632
D_P = 5760
_O_RA = 2304
_O_UB = 2320
_O_QC = 3344
_O_END = 5648

EXP_CLAMP = 80.0

MOE_SUB = 128
MOE_MM = 256
MOE_SLAB = 1536
MOE_TN1 = 512
MOE_TN2 = 512


def _cparams(sem):
    return pltpu.CompilerParams(dimension_semantics=sem, vmem_limit_bytes=VMEM_LIMIT)


def _ada_kernel(c_ref, w_ref, b_ref, o_ref):
    c = c_ref[...]
    s = (c * jax.nn.sigmoid(c)).astype(BF16)
    o_ref[...] = jnp.dot(s, w_ref[...].astype(BF16), preferred_element_type=F32) + b_ref[...]


def _ada_call(c_all, w_ada, b_ada, tn=1024):
    nb, d = c_all.shape
    depth, _, n = w_ada.shape
    return pl.pallas_call(
        _ada_kernel,
        grid=(depth, n // tn),
        in_specs=[
            pl.BlockSpec((nb, d), lambda l, j: (0, 0)),
            pl.BlockSpec((None, d, tn), lambda l, j: (l, 0, j)),
            pl.BlockSpec((None, 1, tn), lambda l, j: (l, 0, j)),
        ],
        out_specs=pl.BlockSpec((None, nb, tn), lambda l, j: (l, 0, j)),
        out_shape=jax.ShapeDtypeStruct((depth, nb, n), F32),
        compiler_params=_cparams(("arbitrary", "arbitrary")),
        name="ada_mod",
    )(c_all, w_ada, b_ada.reshape(depth, 1, n))


def _rms_mod(x, gn, sc, sh):
    ms = jnp.mean(x * x, axis=-1, keepdims=True)
    h = x * lax.rsqrt(ms + EPS) * gn
    return h * (1.0 + sc) + sh


_HI16 = 0xFFFF0000


def _pack_bf16_pairs(x, group):
    outs = []
    for c0 in range(0, x.shape[1], 2 * group):
        lo = lax.bitcast_convert_type(x[:, c0:c0 + group].astype(BF16).astype(F32), jnp.uint32) >> 16
        hi = lax.bitcast_convert_type(x[:, c0 + group:c0 + 2 * group].astype(BF16).astype(F32), jnp.uint32)
        outs.append((hi & jnp.uint32(_HI16)) | lo)
    return outs[0] if len(outs) == 1 else jnp.concatenate(outs, axis=1)


def _unpack_bf16_pairs(u, group):
    lo = lax.bitcast_convert_type(u << 16, F32)
    hi = lax.bitcast_convert_type(u & jnp.uint32(_HI16), F32)
    parts = []
    for c0 in range(0, u.shape[1], group):
        parts += [lo[:, c0:c0 + group], hi[:, c0:c0 + group]]
    return jnp.concatenate(parts, axis=1)


def _resid_norm_kernel(has_y, final, *refs):
    it = iter(refs)
    x_ref = next(it)
    if has_y:
        y_ref, gt_ref, g2_ref = next(it), next(it), next(it)
    gn_ref = next(it)
    if not final:
        sc_ref, sh_ref = next(it), next(it)
    if has_y and not final:
        xo_ref = next(it)
    o_ref = next(it)

    x = x_ref[...]
    if has_y:
        acc = None
        for k in range(TOP_K):
            t = gt_ref[:, k:k + 1] * _unpack_bf16_pairs(y_ref[k], MOE_TN2 // 2)
            acc = t if acc is None else acc + t
        x = x + g2_ref[...] * acc
        if not final:
            xo_ref[...] = x
    if final:
        ms = jnp.mean(x * x, axis=-1, keepdims=True)
        o_ref[...] = x * lax.rsqrt(ms + EPS) * gn_ref[...]
    else:
        o_ref[...] = _rms_mod(x, gn_ref[...], sc_ref[...], sh_ref[...]).astype(o_ref.dtype)


def _mod_spec(mod, tiles_per_group):
    _, r, d = mod.shape
    return pl.BlockSpec((None, r, d), lambda i, j=0: (i // tiles_per_group, 0, 0))


def _resid_norm_call(x, y4, gates, gate2, gnorm, scale, shift, *, tm, tiles_per_group, row_off, final):
    rows, d = x.shape
    has_y = y4 is not None
    row_spec = pl.BlockSpec((tm, d), lambda i: (i, 0))
    in_specs, args = [row_spec], [x]
    if has_y:
        off = row_off // tm
        in_specs += [pl.BlockSpec((TOP_K, tm, d // 2), lambda i: (0, i + off, 0)),
                     pl.BlockSpec((tm, TOP_K), lambda i: (i + off, 0)), _mod_spec(gate2, tiles_per_group)]
        args += [y4, gates, gate2]
    in_specs.append(pl.BlockSpec((1, d), lambda i: (0, 0)))
    args.append(gnorm)
    if not final:
        in_specs += [_mod_spec(scale, tiles_per_group), _mod_spec(shift, tiles_per_group)]
        args += [scale, shift]
    if final:
        out_specs, out_shape = row_spec, jax.ShapeDtypeStruct((rows, d), F32)
    elif has_y:
        out_specs = [row_spec, row_spec]
        out_shape = [jax.ShapeDtypeStruct((rows, d), F32), jax.ShapeDtypeStruct((rows, d), BF16)]
    else:
        out_specs, out_shape = row_spec, jax.ShapeDtypeStruct((rows, d), BF16)
    out = pl.pallas_call(
        functools.partial(_resid_norm_kernel, has_y, final),
        grid=(rows // tm,), in_specs=in_specs, out_specs=out_specs, out_shape=out_shape,
        compiler_params=_cparams(("arbitrary",)),
        name="resid_norm",
    )(*args)
    if final:
        return out
    return (out[0], out[1]) if has_y else (x, out)


def _matmul_kernel(x_ref, w_ref, o_ref):
    o_ref[...] = jnp.dot(x_ref[...], w_ref[...], preferred_element_type=F32)


def _inproj_call(h, w, layer, *, tm, tn):
    rows, d = h.shape
    n = w.shape[2]
    return pl.pallas_call(
        _matmul_kernel,
        grid=(rows // tm, n // tn),
        in_specs=[pl.BlockSpec((tm, d), lambda i, j: (i, 0)),
                  pl.BlockSpec((None, d, tn), lambda i, j: (layer, 0, j))],
        out_specs=pl.BlockSpec((tm, tn), lambda i, j: (i, j)),
        out_shape=jax.ShapeDtypeStruct((rows, n), F32),
        compiler_params=_cparams(("arbitrary", "arbitrary")),
        name="inproj",
    )(h, w)


def _head_masks(width=LANES):
    lane = lax.broadcasted_iota(jnp.int32, (1, width), 1) % LANES
    return lane < DK, lane >= DK


def _expand(x, bb, ls):
    if bb == 1:
        return x
    seq = lax.broadcasted_iota(jnp.int32, (bb * ls, 1), 0) // ls
    return jnp.concatenate([jnp.where(seq == b, x, 0.0) for b in range(bb)], axis=1)


def _dot_nt(a, b):
    return lax.dot_general(a, b, (((1,), (1,)), ((), ())), preferred_element_type=F32)


def _dot_tn(a, b, precision=None):
    return lax.dot_general(a, b, (((0,), (0,)), ((), ())), preferred_element_type=F32,
                           precision=precision)


def _silu(x):
    return x * jax.nn.sigmoid(x)


def _causal_mask(bb, ls):
    t = np.arange(bb * ls)
    same_seg = (t[:, None] // ls) == (t[None, :] // ls)
    return jnp.asarray((same_seg & (t[None, :] <= t[:, None])).astype(np.float32))


def _chain_states(s_in, dec, upd, bb):
    states = [s_in]
    for c in range(bb):
        blk = slice(c * LANES, (c + 1) * LANES)
        states.append(states[-1] * dec[blk] + upd[blk])
    s_all = states[0] if bb == 1 else jnp.concatenate(states[:bb], axis=0)
    return s_all, states[bb]


def _gla_body(q, k, v, g, ra, wg, bg, gout, cmask, s_in, bb, ls, chained):
    rows = bb * ls
    sub = min(GLA_SUB, ls)
    n_sub = ls // sub
    hm = _head_masks()
    seg = lax.broadcasted_iota(jnp.int32, (rows, 1), 0) // ls

    x = jnp.dot(ra.astype(BF16), wg, preferred_element_type=F32) + bg
    log_a = (jnp.minimum(x, 0.0) - jnp.log1p(jnp.exp(-jnp.abs(x)))) * (1.0 / GLA_TAU)
    la_hi = log_a.astype(BF16)
    la_r = log_a - la_hi.astype(F32)
    la_mid = la_r.astype(BF16)
    la_lo = (la_r - la_mid.astype(F32)).astype(BF16)
    b3 = jnp.dot(cmask.astype(BF16), jnp.concatenate([la_hi, la_mid, la_lo], axis=1),
                 preferred_element_type=F32)
    bcum = b3[:, :LANES] + b3[:, LANES:2 * LANES] + b3[:, 2 * LANES:]

    def seg_row(offset):
        out = bcum[offset:offset + 1, :]
        for b in range(1, bb):
            out = jnp.where(seg == b, bcum[b * ls + offset:b * ls + offset + 1, :], out)
        return out

    bend = seg_row(ls - 1)
    bases = [None] + [seg_row(i * sub - 1) for i in range(1, n_sub)]
    sub_id = (lax.broadcasted_iota(jnp.int32, (rows, 1), 0) % ls) // sub
    bs = bcum
    for i in range(1, n_sub):
        bs = jnp.where(sub_id == i, bcum - bases[i], bs)

    qs = q * (DK ** -0.5)
    q_sub = qs * jnp.exp(bs)
    q_seq = qs * jnp.exp(bcum)
    k_end = k * jnp.exp(bend - bcum)

    a_parts, k_parts = [], []
    for i in range(n_sub):
        shift = -bcum if i == 0 else bases[i] - bcum
        k_parts.append((k * jnp.exp(jnp.minimum(shift, EXP_CLAMP))).astype(BF16))
        a_parts.append(jnp.where(sub_id == i, q_sub, 0.0) if n_sub > 1 else q_sub)
    a = a_parts[0] if n_sub == 1 else jnp.concatenate(a_parts, axis=1)
    kk = k_parts[0] if n_sub == 1 else jnp.concatenate(k_parts, axis=1)
    hmw = _head_masks(n_sub * LANES)
    a2 = jnp.concatenate([jnp.where(hmw[0], a, 0.0), jnp.where(hmw[1], a, 0.0)], axis=0).astype(BF16)
    att = _dot_nt(a2, kk)
    att = jnp.where(jnp.concatenate([cmask, cmask], axis=0) > 0.0, att, 0.0)

    vhs = [v[:, h * DV:(h + 1) * DV].astype(BF16) for h in range(2)]
    upd = (_dot_tn(_expand(jnp.where(hm[0], k_end, 0.0), bb, ls).astype(BF16), vhs[0])
           + _dot_tn(_expand(jnp.where(hm[1], k_end, 0.0), bb, ls).astype(BF16), vhs[1]))
    dec_parts = []
    for b in range(bb):
        last = b * ls + ls - 1
        dec_parts.append(jnp.exp(jnp.broadcast_to(bcum[last:last + 1, :], (LANES, LANES)).T))
    dec = dec_parts[0] if bb == 1 else jnp.concatenate(dec_parts, axis=0)
    if chained:
        s_all, s_out = _chain_states(s_in, dec, upd, bb)
    else:
        s_all, s_out = s_in, s_in * dec + upd

    s_bf = s_all.astype(BF16)
    outs = []
    for h in range(2):
        o = jnp.dot(att[h * rows:(h + 1) * rows].astype(BF16), vhs[h], preferred_element_type=F32)
        o = o + jnp.dot(_expand(jnp.where(hm[h], q_seq, 0.0), bb, ls).astype(BF16), s_bf,
                        preferred_element_type=F32)
        o = o * lax.rsqrt(jnp.mean(o * o, axis=-1, keepdims=True) + EPS) * gout
        outs.append(o * _silu(g[:, h * DV:(h + 1) * DV]))
    return jnp.concatenate(outs, axis=1), s_out


def _gla_kernel(cfg, *refs):
    bb, ls, chained, has_s0 = cfg
    (q_ref, k_ref, v_ref, g_ref, ra_ref, wg_ref, bg_ref, gout_ref, cm_ref) = refs[:9]
    if has_s0:
        s0_ref, o_ref, s_ref = refs[9:]
    else:
        o_ref, s_ref = refs[9:]

    @pl.when(pl.program_id(2) == 0)
    def _():
        s_ref[...] = s0_ref[...] if has_s0 else jnp.zeros(s_ref.shape, F32)

    s = s_ref[...].reshape(s_ref.shape[0] * LANES, LANES)
    o, s = _gla_body(q_ref[...], k_ref[...], v_ref[...], g_ref[...], ra_ref[...], wg_ref[...], bg_ref[...],
                     gout_ref[...], cm_ref[...], s, bb, ls, chained)
    o_ref[...] = o.astype(o_ref.dtype)
    s_ref[...] = s.reshape(s_ref.shape)


def _mixer_grid(n_seq, t, bb, ls, chained):
    tl = bb * ls
    if chained:
        steps = t // tl
        return tl, (n_seq, N_PAIR, steps), (lambda o, j, c: o * steps + c), 1
    assert t == ls
    return tl, (n_seq // bb, N_PAIR, 1), (lambda o, j, c: o), bb


def _gla_call(p, wg, bg, gout, s0, *, n_seq, t, bb, ls, chained):
    rows_total = n_seq * t
    tl, grid, row, sb = _mixer_grid(n_seq, t, bb, ls, chained)
    has_s0 = s0 is not None
    cm = _causal_mask(bb, ls)
    r = tl

    def col(width, base):
        return pl.BlockSpec((tl, width), lambda o, j, c: (row(o, j, c), base // width + j))

    const = lambda shape: pl.BlockSpec(shape, lambda o, j, c: (0,) * len(shape))
    st_spec = pl.BlockSpec((sb, None, LANES, LANES), lambda o, j, c: (o, j, 0, 0))
    in_specs = [col(LANES, COL_QA), col(LANES, COL_KA), col(2 * DV, COL_VA), col(2 * DV, COL_GA),
                pl.BlockSpec((tl, LANES), lambda o, j, c: (row(o, j, c), COL_RA // LANES)),
                pl.BlockSpec((LANES, LANES), lambda o, j, c: (0, j)),
                pl.BlockSpec((1, LANES), lambda o, j, c: (0, j)),
                const((1, DV)), const((r, r))]
    args = [p, p, p, p, p, wg, bg, gout, cm]
    if has_s0:
        in_specs.append(st_spec)
        args.append(s0)
    return pl.pallas_call(
        functools.partial(_gla_kernel, (bb, ls, chained, has_s0)),
        grid=grid, in_specs=in_specs,
        out_specs=[pl.BlockSpec((tl, 2 * DV), lambda o, j, c: (row(o, j, c), j)), st_spec],
        out_shape=[jax.ShapeDtypeStruct((rows_total, N_HEADS * DV), BF16),
                   jax.ShapeDtypeStruct((n_seq, N_PAIR, LANES, LANES), F32)],
        compiler_params=_cparams(("arbitrary", "arbitrary", "arbitrary")),
        name="gla_mixer",
    )(*args)


def _ret_tables(bb, ls, pos0):
    rows = bb * ls
    log_g = np.log1p(-np.exp2(-5.0 - np.arange(N_HEADS, dtype=np.float64)))
    t = np.arange(rows)
    n = t % ls
    same_seq = (t[:, None] // ls) == (t[None, :] // ls)
    diff = (n[:, None] - n[None, :]).astype(np.float64)
    causal = same_seq & (diff >= 0)
    dmask = np.where(causal[None], np.exp(np.maximum(diff, 0.0)[None] * log_g[:, None, None]), 0.0)
    lane_head = np.arange(N_PAIR * LANES) // DK
    q_dec = np.exp((n[:, None] + 1.0) * log_g[lane_head][None])
    k_dec = np.exp((ls - 1.0 - n)[:, None] * log_g[lane_head][None])
    c_dec = np.exp(ls * log_g[lane_head])[:, None] * np.ones((1, LANES))
    c_dec = np.tile(c_dec.reshape(N_PAIR, 1, LANES, LANES), (1, bb, 1, 1)).reshape(N_PAIR, bb * LANES, LANES)
    f32 = lambda a: jnp.asarray(a, dtype=F32)
    return f32(dmask), f32(q_dec), f32(k_dec), f32(c_dec)


def _rope_tables(t, pos0):
    half = DK // 2
    inv = ROPE_BASE ** (-np.arange(half, dtype=np.float64) / half)
    ang = (np.arange(t, dtype=np.float64) + pos0)[:, None] * inv[None]
    cos = np.tile(np.cos(ang), (1, 2 * LANES // DK))
    sin = np.tile(np.concatenate([-np.sin(ang), np.sin(ang)], axis=1), (1, LANES // DK))
    return jnp.asarray(cos, dtype=F32), jnp.asarray(sin, dtype=F32)


def _rope(x, cos, sin_signed):
    lane = lax.broadcasted_iota(jnp.int32, (1, LANES), 1) % DK
    swapped = jnp.where(lane < DK // 2, pltpu.roll(x, LANES - DK // 2, 1), pltpu.roll(x, DK // 2, 1))
    return x * cos + swapped * sin_signed


def _ret_body(q, k, v, g, cos, sin, dmask, q_dec, k_dec, c_dec, gn, s_in, bb, ls, chained):
    hm = _head_masks()
    qr = _rope(q, cos, sin)
    kr = _rope(k, cos, sin) * (DK ** -0.5)
    kr_bf = kr.astype(BF16)
    q_in = qr * q_dec
    k_end = kr * k_dec
    vhs = [v[:, h * DV:(h + 1) * DV].astype(BF16) for h in range(2)]
    upd = (_dot_tn(_expand(jnp.where(hm[0], k_end, 0.0), bb, ls).astype(BF16), vhs[0])
           + _dot_tn(_expand(jnp.where(hm[1], k_end, 0.0), bb, ls).astype(BF16), vhs[1]))
    if chained:
        s_all, s_out = _chain_states(s_in, c_dec, upd, bb)
    else:
        s_all, s_out = s_in, s_in * c_dec + upd
    s_bf = s_all.astype(BF16)
    outs = []
    for h in range(2):
        att = _dot_nt(jnp.where(hm[h], qr, 0.0).astype(BF16), kr_bf) * dmask[h]
        o = jnp.dot(att.astype(BF16), vhs[h], preferred_element_type=F32)
        o = o + jnp.dot(_expand(jnp.where(hm[h], q_in, 0.0), bb, ls).astype(BF16), s_bf,
                        preferred_element_type=F32)
        d = o - jnp.mean(o, axis=-1, keepdims=True)
        var = jnp.mean(d * d, axis=-1, keepdims=True)
        o = d * lax.rsqrt(var + EPS) * gn[:, h * DV:(h + 1) * DV]
        outs.append(o * _silu(g[:, h * DV:(h + 1) * DV]))
    return jnp.concatenate(outs, axis=1), s_out


def _ret_kernel(cfg, *refs):
    bb, ls, chained, has_s0 = cfg
    (q_ref, k_ref, v_ref, g_ref, cos_ref, sin_ref, dm_ref, qd_ref, kd_ref, cd_ref, gn_ref) = refs[:11]
    if has_s0:
        s0_ref, o_ref, s_ref = refs[11:]
    else:
        o_ref, s_ref = refs[11:]

    @pl.when(pl.program_id(2) == 0)
    def _():
        s_ref[...] = s0_ref[...] if has_s0 else jnp.zeros(s_ref.shape, F32)

    s = s_ref[...].reshape(s_ref.shape[0] * LANES, LANES)
    o, s = _ret_body(q_ref[...], k_ref[...], v_ref[...], g_ref[...], cos_ref[...], sin_ref[...], dm_ref[...],
                     qd_ref[...], kd_ref[...], cd_ref[...], gn_ref[...], s, bb, ls, chained)
    o_ref[...] = o.astype(o_ref.dtype)
    s_ref[...] = s.reshape(s_ref.shape)


def _ret_call(p, gn, s0, *, n_seq, t, bb, ls, chained, pos0):
    rows_total = n_seq * t
    tl, grid, row, sb = _mixer_grid(n_seq, t, bb, ls, chained)
    has_s0 = s0 is not None
    dmask, q_dec, k_dec, c_dec = _ret_tables(bb, ls, pos0)
    cos, sin = _rope_tables(t, pos0)
    r = tl
    if chained:
        rope_spec = pl.BlockSpec((tl, LANES), lambda o, j, c: (c, 0))
    else:
        cos, sin = jnp.tile(cos, (bb, 1)), jnp.tile(sin, (bb, 1))
        rope_spec = pl.BlockSpec((tl, LANES), lambda o, j, c: (0, 0))

    def col(width, base):
        return pl.BlockSpec((tl, width), lambda o, j, c: (row(o, j, c), base // width + j))

    st_spec = pl.BlockSpec((sb, None, LANES, LANES), lambda o, j, c: (o, j, 0, 0))
    in_specs = [col(LANES, COL_QC), col(LANES, COL_KC), col(2 * DV, COL_VC), col(2 * DV, COL_GC),
                rope_spec, rope_spec,
                pl.BlockSpec((2, r, r), lambda o, j, c: (j, 0, 0)),
                pl.BlockSpec((r, LANES), lambda o, j, c: (0, j)),
                pl.BlockSpec((r, LANES), lambda o, j, c: (0, j)),
                pl.BlockSpec((None, bb * LANES, LANES), lambda o, j, c: (j, 0, 0)),
                pl.BlockSpec((1, 2 * DV), lambda o, j, c: (0, j))]
    args = [p, p, p, p, cos, sin, dmask, q_dec, k_dec, c_dec, gn]
    if has_s0:
        in_specs.append(st_spec)
        args.append(s0)
    return pl.pallas_call(
        functools.partial(_ret_kernel, (bb, ls, chained, has_s0)),
        grid=grid, in_specs=in_specs,
        out_specs=[pl.BlockSpec((tl, 2 * DV), lambda o, j, c: (row(o, j, c), j)), st_spec],
        out_shape=[jax.ShapeDtypeStruct((rows_total, N_HEADS * DV), BF16),
                   jax.ShapeDtypeStruct((n_seq, N_PAIR, LANES, LANES), F32)],
        compiler_params=_cparams(("arbitrary", "arbitrary", "arbitrary")),
        name="ret_mixer",
    )(*args)


def _gelu(x):
    return 0.5 * x * (1.0 + jnp.tanh(0.7978845608028654 * (x + 0.044715 * x * x * x)))


def _cmlp_kernel(n_tiles, want_v, u_ref, v_ref, w_ref, bias_ref, lg_ref, lb_ref, *out_refs):
    o_ref = out_refs[0]
    for ti in range(n_tiles):
        sl = slice(ti * CHUNK_B, (ti + 1) * CHUNK_B)
        u = _gelu(u_ref[sl, :])
        vv = _gelu(v_ref[sl, :])
        vn_parts, z_parts = [], []
        for gi in range(G_B):
            vg = vv[:, gi * CG_B:(gi + 1) * CG_B]
            d = vg - jnp.mean(vg, axis=-1, keepdims=True)
            vn = d * lax.rsqrt(jnp.mean(d * d, axis=-1, keepdims=True) + EPS)
            vn = vn * lg_ref[:, gi * CG_B:(gi + 1) * CG_B] + lb_ref[:, gi * CG_B:(gi + 1) * CG_B]
            vn_parts.append(vn)
            z_parts.append(jnp.dot(w_ref[gi], vn.astype(BF16), preferred_element_type=F32))
        z = jnp.concatenate(z_parts, axis=1) + bias_ref[...]
        o_ref[sl, :] = (u * z).astype(o_ref.dtype)
        if want_v:
            out_refs[1][sl, :] = jnp.concatenate(vn_parts, axis=1)


def _cmlp_call(p, w_mix, bias_tile, ln_g, ln_b, *, n_tiles, want_v):
    rows = p.shape[0]
    tl = n_tiles * CHUNK_B
    const = lambda shape: pl.BlockSpec(shape, lambda i: (0,) * len(shape))
    out_specs = [pl.BlockSpec((tl, W_B), lambda i: (i, 0))]
    out_shape = [jax.ShapeDtypeStruct((rows, W_B), BF16)]
    if want_v:
        out_specs.append(pl.BlockSpec((tl, W_B), lambda i: (i, 0)))
        out_shape.append(jax.ShapeDtypeStruct((rows, W_B), F32))
    out = pl.pallas_call(
        functools.partial(_cmlp_kernel, n_tiles, want_v),
        grid=(rows // tl,),
        in_specs=[pl.BlockSpec((tl, W_B), lambda i: (i, COL_UB // W_B)),
                  pl.BlockSpec((tl, W_B), lambda i: (i, COL_VB // W_B)),
                  const((G_B, CHUNK_B, CHUNK_B)), const((CHUNK_B, W_B)), const((1, W_B)), const((1, W_B))],
        out_specs=out_specs, out_shape=out_shape,
        compiler_params=_cparams(("arbitrary",)),
        name="cmlp_mixer",
    )(p, p, w_mix, bias_tile, ln_g, ln_b)
    return (out[0], out[1]) if want_v else (out[0], None)


def _outproj_kernel(oa_ref, ob_ref, oc_ref, w_ref, x_ref, g1_ref, gn_ref, sc_ref, sh_ref,
                    wr_ref, br_ref, xo_ref, h_ref, lg_ref):
    wa = N_HEADS * DV
    acc = jnp.dot(oa_ref[...], w_ref[0:wa, :], preferred_element_type=F32)
    acc = acc + jnp.dot(ob_ref[...], w_ref[wa:wa + W_B, :], preferred_element_type=F32)
    acc = acc + jnp.dot(oc_ref[...], w_ref[wa + W_B:, :], preferred_element_type=F32)
    x = x_ref[...] + g1_ref[...] * acc
    xo_ref[...] = x
    h = _rms_mod(x, gn_ref[...], sc_ref[...], sh_ref[...])
    h_ref[...] = _pack_bf16_pairs(h, h.shape[1] // 2)
    hi = h.astype(BF16)
    lo = (h - hi.astype(F32)).astype(BF16)
    r1 = jnp.dot(hi, wr_ref[...], preferred_element_type=F32)
    r2 = jnp.dot(lo, wr_ref[:, :LANES], preferred_element_type=F32)
    lg_ref[...] = r1[:, :LANES] + r1[:, LANES:] + r2 + br_ref[...]


def _outproj_call(oa, ob, oc, w_out, layer, x, gate1, gnorm, scale, shift, w_router, b_router, *, tm,
                  tiles_per_group):
    rows, d = x.shape
    rspec = lambda width: pl.BlockSpec((tm, width), lambda i: (i, 0))
    const = lambda shape: pl.BlockSpec(shape, lambda i: (0,) * len(shape))
    ms = lambda m: _mod_spec(m, tiles_per_group)
    return pl.pallas_call(
        _outproj_kernel,
        grid=(rows // tm,),
        in_specs=[rspec(oa.shape[1]), rspec(ob.shape[1]), rspec(oc.shape[1]),
                  pl.BlockSpec((None,) + w_out.shape[1:], lambda i: (layer, 0, 0)),
                  rspec(d), ms(gate1), const((1, d)), ms(scale), ms(shift),
                  const(w_router.shape), const(b_router.shape)],
        out_specs=[rspec(d), rspec(d // 2), rspec(LANES)],
        out_shape=[jax.ShapeDtypeStruct((rows, d), F32), jax.ShapeDtypeStruct((rows, d // 2), jnp.uint32),
                   jax.ShapeDtypeStruct((rows, LANES), F32)],
        compiler_params=_cparams(("arbitrary",)),
        name="outproj_norm_router",
    )(oa, ob, oc, w_out, x, gate1, gnorm, scale, shift, w_router, b_router)


def _moe_in_kernel(lay_ref, se_ref, nb_ref, so_ref, x_ref, wg_ref, wu_ref, bg_ref, bu_ref, h_ref,
                   x_s, wg_s, wu_s):
    nb = nb_ref[pl.program_id(0)]

    @pl.when(pl.program_id(1) == 0)
    def _():
        for rb in range(MOE_SLAB // MOE_SUB):
            sl = slice(rb * MOE_SUB, (rb + 1) * MOE_SUB)

            @pl.when(rb < nb)
            def _():
                x_s[sl, :] = _unpack_bf16_pairs(x_ref[sl, :], x_ref.shape[1]).astype(BF16)

    wg_s[...] = wg_ref[...].astype(BF16)
    wu_s[...] = wu_ref[...].astype(BF16)
    for n in range(1, MOE_SLAB // MOE_SUB + 1):
        rows = n * MOE_SUB

        @pl.when(nb == n)
        def _():
            for r0 in range(0, rows, MOE_MM):
                sl = slice(r0, min(r0 + MOE_MM, rows))
                xb = x_s[sl, :]
                gt = jnp.dot(xb, wg_s[...], preferred_element_type=F32) + bg_ref[...]
                up = jnp.dot(xb, wu_s[...], preferred_element_type=F32) + bu_ref[...]
                gt = jnp.minimum(gt, SWIGLU_LIMIT)
                up = jnp.clip(up, -SWIGLU_LIMIT, SWIGLU_LIMIT)
                glu = gt * jax.nn.sigmoid(SWIGLU_ALPHA * gt)
                h_ref[sl, :] = ((up + 1.0) * glu).astype(h_ref.dtype)
            if rows < MOE_SLAB:
                h_ref[rows:, :] = jnp.zeros((MOE_SLAB - rows, h_ref.shape[1]), h_ref.dtype)


def _moe_out_kernel(lay_ref, se_ref, nb_ref, so_ref, h_ref, w_ref, b_ref, y_ref, w_s):
    nb = nb_ref[pl.program_id(0)]
    w_s[...] = w_ref[...].astype(BF16)
    for n in range(1, MOE_SLAB // MOE_SUB + 1):
        rows = n * MOE_SUB

        @pl.when(nb == n)
        def _():
            y = jnp.dot(h_ref[0:rows, :], w_s[...], preferred_element_type=F32) + b_ref[...]
            y_ref[0:rows, :] = _pack_bf16_pairs(y, MOE_TN2 // 2)
            if rows < MOE_SLAB:
                y_ref[rows:, :] = jnp.zeros((MOE_SLAB - rows, y_ref.shape[1]), y_ref.dtype)


def _moe_calls(x_rows, n_used, slab_e, slab_nb, slab_off, w_in, b_in, w_out, b_out, layer):
    n_slab = slab_e.shape[0]
    r = MOE_SLAB
    dh = x_rows.shape[1]
    d = 2 * dh
    nt1 = D_FF // MOE_TN1
    nt2 = d // MOE_TN2
    up_off = D_FF // MOE_TN1

    lay = jnp.full((1,), layer, jnp.int32)

    def wspec(k, tn, off=0):
        return pl.BlockSpec((None, None, k, tn), lambda s, j, ly, se, nb, so: (ly[0], se[s], 0, off + j))

    h = pl.pallas_call(
        _moe_in_kernel,
        grid_spec=pltpu.PrefetchScalarGridSpec(
            num_scalar_prefetch=4, grid=(n_used, nt1),
            in_specs=[
                pl.BlockSpec((pl.Element(r), pl.Element(dh)),
                             lambda s, j, ly, se, nb, so: (so[s] * MOE_SUB, 0)),
                wspec(d, MOE_TN1), wspec(d, MOE_TN1, up_off), wspec(1, MOE_TN1), wspec(1, MOE_TN1, up_off),
            ],
            out_specs=pl.BlockSpec((None, r, MOE_TN1), lambda s, j, ly, se, nb, so: (s, 0, j)),
            scratch_shapes=[pltpu.VMEM((r, d), BF16), pltpu.VMEM((d, MOE_TN1), BF16),
                            pltpu.VMEM((d, MOE_TN1), BF16)]),
        out_shape=jax.ShapeDtypeStruct((n_slab, r, D_FF), BF16),
        compiler_params=_cparams(("arbitrary", "arbitrary")),
        name="moe_glu",
    )(lay, slab_e, slab_nb, slab_off, x_rows, w_in, w_in, b_in, b_in)

    y = pl.pallas_call(
        _moe_out_kernel,
        grid_spec=pltpu.PrefetchScalarGridSpec(
            num_scalar_prefetch=4, grid=(n_used, nt2),
            in_specs=[
                pl.BlockSpec((None, r, D_FF), lambda s, j, ly, se, nb, so: (s, 0, 0)),
                wspec(D_FF, MOE_TN2), wspec(1, MOE_TN2),
            ],
            out_specs=pl.BlockSpec((None, r, MOE_TN2 // 2), lambda s, j, ly, se, nb, so: (s, 0, j)),
            scratch_shapes=[pltpu.VMEM((D_FF, MOE_TN2), BF16)]),
        out_shape=jax.ShapeDtypeStruct((n_slab, r, dh), jnp.uint32),
        compiler_params=_cparams(("arbitrary", "arbitrary")),
        name="moe_down",
    )(lay, slab_e, slab_nb, slab_off, h, w_out, b_out)
    return y


def _rows(a, idx):
    return a.at[idx].get(mode="promise_in_bounds")


def _count_le(bounds, x):
    return jnp.sum((bounds <= x[..., None]).astype(jnp.int32), axis=-1)


def _n_slabs(n_assign):
    return N_EXPERTS + (n_assign + N_EXPERTS * (MOE_SUB - 1)) // MOE_SLAB


def _route(logits, n_slab):
    m = logits.shape[0]
    top_val, top_idx = lax.top_k(logits, TOP_K)
    gates = jax.nn.softmax(top_val, axis=-1)
    a = m * TOP_K
    flat_e = top_idx.reshape(a).astype(jnp.int32)
    ar = jnp.arange(a, dtype=jnp.int32)
    sorted_e, order = lax.sort((flat_e, ar), num_keys=1, is_stable=True)
    counts = jnp.bincount(flat_e, length=N_EXPERTS).astype(jnp.int32)
    start = jnp.cumsum(counts) - counts
    rank = ar - start[sorted_e]
    padded = (counts + MOE_SUB - 1) // MOE_SUB * MOE_SUB
    nslab_e = (padded + MOE_SLAB - 1) // MOE_SLAB
    slab_end = jnp.cumsum(nslab_e)
    slab_start = slab_end - nslab_e
    dest = (slab_start[sorted_e] + rank // MOE_SLAB) * MOE_SLAB + rank % MOE_SLAB
    _, pos = lax.sort((order, dest), num_keys=1)
    pos = pos.reshape(m, TOP_K)
    n_used = slab_end[-1]
    sid = jnp.arange(n_slab, dtype=jnp.int32)
    slab_e = jnp.minimum(_count_le(slab_end, sid), N_EXPERTS - 1)
    local = sid - slab_start[slab_e]
    slab_nb = (jnp.clip(padded[slab_e] - local * MOE_SLAB, 0, MOE_SLAB) // MOE_SUB).astype(jnp.int32)
    cend = jnp.cumsum(padded)
    cstart = cend - padded
    slab_off = ((cstart[slab_e] + local * MOE_SLAB) // MOE_SUB).astype(jnp.int32)
    n_rows = a + N_EXPERTS * MOE_SUB + MOE_SLAB
    rid = jnp.arange(n_rows, dtype=jnp.int32)
    e_r = jnp.minimum(_count_le(cend, rid), N_EXPERTS - 1)
    rk = rid - cstart[e_r]
    src = jnp.clip(start[e_r] + rk, 0, a - 1)
    row_tok = jnp.where(rk < counts[e_r], _rows(order, src) // TOP_K, rid % m)
    return gates, pos, row_tok, n_used.astype(jnp.int32), slab_e, slab_nb, slab_off


def _moe(h2u, logits, w_in, b_in, w_out, b_out, layer):
    m, dh = h2u.shape
    n_slab = _n_slabs(m * TOP_K)
    gates, pos, row_tok, n_used, slab_e, slab_nb, slab_off = _route(logits, n_slab)
    x_rows = _rows(h2u, row_tok)
    y = _moe_calls(x_rows, n_used, slab_e, slab_nb, slab_off, w_in, b_in.reshape(b_in.shape[0], N_EXPERTS, 1, -1),
                   w_out, b_out.reshape(b_out.shape[0], N_EXPERTS, 1, -1), layer)
    y4 = _rows(y.reshape(n_slab * MOE_SLAB, dh), pos.T.reshape(-1))
    return y4.reshape(TOP_K, m, dh), gates


def _prep_w_in(w_in):
    depth, d, _ = w_in.shape
    w = w_in.astype(BF16)
    pad = jnp.zeros((depth, d, D_P - COL_RA - GLA_LOWRANK), BF16)
    return jnp.concatenate([w[:, :, _O_UB:_O_QC], w[:, :, :_O_RA], w[:, :, _O_QC:_O_END],
                            w[:, :, _O_RA:_O_UB], pad], axis=-1)


def _prep_cmlp(w_spatial, b_spatial, t):
    tri = jnp.tril(jnp.ones((CHUNK_B, CHUNK_B), bool))
    w = jnp.where(tri[None, None], w_spatial, 0.0)
    if t < CHUNK_B:
        reps = CHUNK_B // t
        eye = jnp.eye(reps, dtype=w.dtype)
        w = jnp.einsum("ab,lgij->lgaibj", eye, w[:, :, :t, :t]).reshape(w.shape[0], G_B, CHUNK_B, CHUNK_B)
        b = jnp.tile(b_spatial[:, :, :t], (1, 1, reps))
    else:
        b = b_spatial
    bias = jnp.repeat(jnp.swapaxes(b, 1, 2), CG_B, axis=2)
    return w.astype(BF16), bias.astype(F32)


def kernel(x_prompt, x_sample, c_prompt, c_sample, state_gla, state_ret, w_ada, b_ada, g_norm1,
           g_norm2, w_in, w_gla_gate, b_gla_gate, g_gla_out, ln_g_cmlp, ln_b_cmlp, w_spatial,
           b_spatial, g_ret_out, w_out, w_router, b_router, w_e_in, b_e_in, w_e_out, b_e_out, g_final):
    bp, tp, d = x_prompt.shape
    bs, ts, _ = x_sample.shape
    rows_p, rows_s = bp * tp, bs * ts
    tmn_p, tmn_s = 256, 256
    tmo_p, tmo_s = 512, 256
    tm_in, tn_in = 1024, 1152
    dec_bb = LANES // ts

    n_c = bp + bs
    n_c_pad = -(-n_c // SUBLANES) * SUBLANES
    c_all = jnp.concatenate([c_prompt, c_sample, jnp.zeros((n_c_pad - n_c, d), F32)], axis=0)
    mod = _ada_call(c_all, w_ada, b_ada).reshape(DEPTH, n_c_pad, N_MOD, d)
    mod_p = jnp.transpose(mod[:, :bp], (0, 2, 1, 3)).reshape(DEPTH, N_MOD, bp, 1, d)
    mod_s = jnp.repeat(jnp.transpose(mod[:, bp:n_c], (0, 2, 1, 3)), ts, axis=2)

    w_in_r = _prep_w_in(w_in)
    w_out_bf = w_out.astype(BF16)
    wg_pad = jnp.pad(w_gla_gate, ((0, 0), (0, LANES - GLA_LOWRANK), (0, 0))).astype(BF16)
    w_router_pad = jnp.pad(w_router, ((0, 0), (0, 0), (0, LANES - N_EXPERTS)))
    w_router_hi = w_router_pad.astype(BF16)
    w_router_lo = (w_router_pad - w_router_hi.astype(F32)).astype(BF16)
    w_router_2 = jnp.concatenate([w_router_hi, w_router_lo], axis=-1)
    b_router_pad = jnp.pad(b_router, ((0, 0), (0, LANES - N_EXPERTS))).reshape(DEPTH, 1, LANES)
    wmix_p, bias_p = _prep_cmlp(w_spatial, b_spatial, CHUNK_B)
    wmix_s, bias_s = _prep_cmlp(w_spatial, b_spatial, ts)

    xp = x_prompt.reshape(rows_p, d)
    xs = x_sample.reshape(rows_s, d)
    st_gla = state_gla.reshape(DEPTH, bs, N_PAIR, LANES, LANES)
    st_ret = state_ret.reshape(DEPTH, bs, N_PAIR, LANES, LANES)
    y4 = gates = None
    outs = {k: [] for k in ("gla_p", "ret_p", "gla_s", "ret_s", "v_s")}

    for l in range(DEPTH):
        row = lambda a: a[l].reshape(1, -1)
        mp = lambda i: mod_p[l, i]
        msf = lambda i, tm: mod_s[l, i].reshape(rows_s // tm, tm, d)
        g2p = mod_p[l - 1, 5] if l > 0 else None
        g2s = mod_s[l - 1, 5].reshape(rows_s // tmn_s, tmn_s, d) if l > 0 else None

        xp, hp = _resid_norm_call(xp, y4, gates, g2p, row(g_norm1), mp(1), mp(0), tm=tmn_p,
                                  tiles_per_group=tp // tmn_p, row_off=0, final=False)
        xs, hs = _resid_norm_call(xs, y4, gates, g2s, row(g_norm1), msf(1, tmn_s), msf(0, tmn_s), tm=tmn_s,
                                  tiles_per_group=1, row_off=rows_p, final=False)
        pp = _inproj_call(hp, w_in_r, l, tm=tm_in, tn=tn_in)
        ps = _inproj_call(hs, w_in_r, l, tm=tm_in, tn=tn_in)

        bg = row(b_gla_gate)
        gout = row(g_gla_out)
        gret = row(g_ret_out)
        oa_p, sa_p = _gla_call(pp, wg_pad[l], bg, gout, None, n_seq=bp, t=tp, bb=4, ls=GLA_CHUNK, chained=True)
        oa_s, sa_s = _gla_call(ps, wg_pad[l], bg, gout, st_gla[l], n_seq=bs, t=ts, bb=dec_bb, ls=ts,
                               chained=False)
        oc_p, sc_p = _ret_call(pp, gret, None, n_seq=bp, t=tp, bb=2, ls=RET_CHUNK, chained=True, pos0=0)
        oc_s, sc_s = _ret_call(ps, gret, st_ret[l], n_seq=bs, t=ts, bb=dec_bb, ls=ts, chained=False,
                               pos0=PAST_LEN)
        ob_p, _ = _cmlp_call(pp, wmix_p[l], bias_p[l], row(ln_g_cmlp), row(ln_b_cmlp), n_tiles=4, want_v=False)
        ob_s, v_s = _cmlp_call(ps, wmix_s[l], bias_s[l], row(ln_g_cmlp), row(ln_b_cmlp), n_tiles=4, want_v=True)

        xp, h2p, lgp = _outproj_call(oa_p, ob_p, oc_p, w_out_bf, l, xp, mp(2), row(g_norm2), mp(4), mp(3),
                                     w_router_2[l], b_router_pad[l], tm=tmo_p, tiles_per_group=tp // tmo_p)
        xs, h2s, lgs = _outproj_call(oa_s, ob_s, oc_s, w_out_bf, l, xs, msf(2, tmo_s), row(g_norm2),
                                     msf(4, tmo_s), msf(3, tmo_s), w_router_2[l], b_router_pad[l],
                                     tm=tmo_s, tiles_per_group=1)

        h2 = jnp.concatenate([h2p, h2s], axis=0)
        logits = jnp.concatenate([lgp, lgs], axis=0)[:, :N_EXPERTS]
        y4, gates = _moe(h2, logits, w_e_in, b_e_in, w_e_out, b_e_out, l)

        outs["gla_p"].append(sa_p.reshape(bp, N_HEADS, DK, DV))
        outs["ret_p"].append(sc_p.reshape(bp, N_HEADS, DK, DV))
        outs["gla_s"].append(sa_s.reshape(bs, N_HEADS, DK, DV))
        outs["ret_s"].append(sc_s.reshape(bs, N_HEADS, DK, DV))
        outs["v_s"].append(v_s.reshape(bs, ts, W_B))

    yp = _resid_norm_call(xp, y4, gates, mod_p[DEPTH - 1, 5], g_final.reshape(1, d), None, None, tm=tmn_p,
                          tiles_per_group=tp // tmn_p, row_off=0, final=True)
    ys = _resid_norm_call(xs, y4, gates, mod_s[DEPTH - 1, 5].reshape(rows_s // tmn_s, tmn_s, d),
                          g_final.reshape(1, d), None, None, tm=tmn_s, tiles_per_group=1, row_off=rows_p,
                          final=True)
    return (yp.reshape(bp, tp, d), ys.reshape(bs, ts, d), jnp.stack(outs["gla_p"]), jnp.stack(outs["ret_p"]),
            jnp.stack(outs["gla_s"]), jnp.stack(outs["ret_s"]), jnp.stack(outs["v_s"]))
```

```python
import functools

import numpy as np
import jax
import jax.numpy as jnp
from jax import lax
from jax.experimental import pallas as pl
from jax.experimental.pallas import tpu as pltpu

F32 = jnp.float32
BF16 = jnp.bfloat16
HIGHEST = lax.Precision.HIGHEST

D_MODEL = 2048
DEPTH = 4
N_HEADS = 6
N_PAIR = N_HEADS // 2
DK = 64
DV = 128
GLA_LOWRANK = 16
GLA_TAU = 16.0
GLA_CHUNK = 64
GLA_SUB = 16
G_B = 4
CG_B = 128
W_B = G_B * CG_B
CHUNK_B = 128
RET_CHUNK = 128
ROPE_BASE = 10000.0
N_EXPERTS = 32
TOP_K = 4
D_FF = D_MODEL
SWIGLU_LIMIT = 7.0
SWIGLU_ALPHA = 1.702
N_MOD = 6
EPS = 1e-6
PAST_LEN = 16384

LANES = 128
SUBLANES = 8
VMEM_LIMIT = 56 * 1024 * 1024

COL_UB = 0
COL_VB = 512
COL_QA = 1024
COL_KA = 1408
COL_VA = 1792
COL_GA = 2560
COL_QC = 3328
COL_KC = 3712
COL_VC = 4096
COL_GC = 4864
COL_RA = 5632
D_P = 5760
_O_RA = 2304
_O_UB = 2320
_O_QC = 3344
_O_END = 5648

EXP_CLAMP = 80.0

MOE_SUB = 256
MOE_SLAB = 1536
MOE_TN1 = 512
MOE_TN2 = 512


def _cparams(sem):
    return pltpu.CompilerParams(dimension_semantics=sem, vmem_limit_bytes=VMEM_LIMIT)


def _ada_kernel(c_ref, w_ref, b_ref, o_ref):
    c = c_ref[...]
    s = (c * jax.nn.sigmoid(c)).astype(BF16)
    o_ref[...] = jnp.dot(s, w_ref[...].astype(BF16), preferred_element_type=F32) + b_ref[...]


def _ada_call(c_all, w_ada, b_ada, tn=1024):
    nb, d = c_all.shape
    depth, _, n = w_ada.shape
    return pl.pallas_call(
        _ada_kernel,
        grid=(depth, n // tn),
        in_specs=[
            pl.BlockSpec((nb, d), lambda l, j: (0, 0)),
            pl.BlockSpec((None, d, tn), lambda l, j: (l, 0, j)),
            pl.BlockSpec((None, 1, tn), lambda l, j: (l, 0, j)),
        ],
        out_specs=pl.BlockSpec((None, nb, tn), lambda l, j: (l, 0, j)),
        out_shape=jax.ShapeDtypeStruct((depth, nb, n), F32),
        compiler_params=_cparams(("arbitrary", "arbitrary")),
        name="ada_mod",
    )(c_all, w_ada, b_ada.reshape(depth, 1, n))


def _rms_mod(x, gn, sc, sh):
    ms = jnp.mean(x * x, axis=-1, keepdims=True)
    h = x * lax.rsqrt(ms + EPS) * gn
    return h * (1.0 + sc) + sh


_HI16 = 0xFFFF0000


def _pack_bf16_pairs(x, group):
    outs = []
    for c0 in range(0, x.shape[1], 2 * group):
        lo = lax.bitcast_convert_type(x[:, c0:c0 + group].astype(BF16).astype(F32), jnp.uint32) >> 16
        hi = lax.bitcast_convert_type(x[:, c0 + group:c0 + 2 * group].astype(BF16).astype(F32), jnp.uint32)
        outs.append((hi & jnp.uint32(_HI16)) | lo)
    return outs[0] if len(outs) == 1 else jnp.concatenate(outs, axis=1)


def _unpack_bf16_pairs(u, group):
    lo = lax.bitcast_convert_type(u << 16, F32)
    hi = lax.bitcast_convert_type(u & jnp.uint32(_HI16), F32)
    parts = []
    for c0 in range(0, u.shape[1], group):
        parts += [lo[:, c0:c0 + group], hi[:, c0:c0 + group]]
    return jnp.concatenate(parts, axis=1)


def _resid_norm_kernel(has_y, final, *refs):
    it = iter(refs)
    x_ref = next(it)
    if has_y:
        y_ref, gt_ref, g2_ref = next(it), next(it), next(it)
    gn_ref = next(it)
    if not final:
        sc_ref, sh_ref = next(it), next(it)
    if has_y and not final:
        xo_ref = next(it)
    o_ref = next(it)

    x = x_ref[...]
    if has_y:
        acc = None
        for k in range(TOP_K):
            t = gt_ref[:, k:k + 1] * _unpack_bf16_pairs(y_ref[k], MOE_TN2 // 2)
            acc = t if acc is None else acc + t
        x = x + g2_ref[...] * acc
        if not final:
            xo_ref[...] = x
    if final:
        ms = jnp.mean(x * x, axis=-1, keepdims=True)
        o_ref[...] = x * lax.rsqrt(ms + EPS) * gn_ref[...]
    else:
        o_ref[...] = _rms_mod(x, gn_ref[...], sc_ref[...], sh_ref[...]).astype(o_ref.dtype)


def _mod_spec(mod, tiles_per_group):
    _, r, d = mod.shape
    return pl.BlockSpec((None, r, d), lambda i, j=0: (i // tiles_per_group, 0, 0))


def _resid_norm_call(x, y4, gates, gate2, gnorm, scale, shift, *, tm, tiles_per_group, row_off, final):
    rows, d = x.shape
    has_y = y4 is not None
    row_spec = pl.BlockSpec((tm, d), lambda i: (i, 0))
    in_specs, args = [row_spec], [x]
    if has_y:
        off = row_off // tm
        in_specs += [pl.BlockSpec((TOP_K, tm, d // 2), lambda i: (0, i + off, 0)),
                     pl.BlockSpec((tm, TOP_K), lambda i: (i + off, 0)), _mod_spec(gate2, tiles_per_group)]
        args += [y4, gates, gate2]
    in_specs.append(pl.BlockSpec((1, d), lambda i: (0, 0)))
    args.append(gnorm)
    if not final:
        in_specs += [_mod_spec(scale, tiles_per_group), _mod_spec(shift, tiles_per_group)]
        args += [scale, shift]
    if final:
        out_specs, out_shape = row_spec, jax.ShapeDtypeStruct((rows, d), F32)
    elif has_y:
        out_specs = [row_spec, row_spec]
        out_shape = [jax.ShapeDtypeStruct((rows, d), F32), jax.ShapeDtypeStruct((rows, d), BF16)]
    else:
        out_specs, out_shape = row_spec, jax.ShapeDtypeStruct((rows, d), BF16)
    out = pl.pallas_call(
        functools.partial(_resid_norm_kernel, has_y, final),
        grid=(rows // tm,), in_specs=in_specs, out_specs=out_specs, out_shape=out_shape,
        compiler_params=_cparams(("arbitrary",)),
        name="resid_norm",
    )(*args)
    if final:
        return out
    return (out[0], out[1]) if has_y else (x, out)


def _matmul_kernel(x_ref, w_ref, o_ref):
    o_ref[...] = jnp.dot(x_ref[...], w_ref[...], preferred_element_type=F32)


def _inproj_call(h, w, layer, *, tm, tn):
    rows, d = h.shape
    n = w.shape[2]
    return pl.pallas_call(
        _matmul_kernel,
        grid=(rows // tm, n // tn),
        in_specs=[pl.BlockSpec((tm, d), lambda i, j: (i, 0)),
                  pl.BlockSpec((None, d, tn), lambda i, j: (layer, 0, j))],
        out_specs=pl.BlockSpec((tm, tn), lambda i, j: (i, j)),
        out_shape=jax.ShapeDtypeStruct((rows, n), F32),
        compiler_params=_cparams(("arbitrary", "arbitrary")),
        name="inproj",
    )(h, w)


def _head_masks(width=LANES):
    lane = lax.broadcasted_iota(jnp.int32, (1, width), 1) % LANES
    return lane < DK, lane >= DK


def _expand(x, bb, ls):
    if bb == 1:
        return x
    seq = lax.broadcasted_iota(jnp.int32, (bb * ls, 1), 0) // ls
    return jnp.concatenate([jnp.where(seq == b, x, 0.0) for b in range(bb)], axis=1)


def _dot_nt(a, b):
    return lax.dot_general(a, b, (((1,), (1,)), ((), ())), preferred_element_type=F32)


def _dot_tn(a, b, precision=None):
    return lax.dot_general(a, b, (((0,), (0,)), ((), ())), preferred_element_type=F32,
                           precision=precision)


def _silu(x):
    return x * jax.nn.sigmoid(x)


def _causal_mask(bb, ls):
    t = np.arange(bb * ls)
    same_seg = (t[:, None] // ls) == (t[None, :] // ls)
    return jnp.asarray((same_seg & (t[None, :] <= t[:, None])).astype(np.float32))


def _chain_states(s_in, dec, upd, bb):
    states = [s_in]
    for c in range(bb):
        blk = slice(c * LANES, (c + 1) * LANES)
        states.append(states[-1] * dec[blk] + upd[blk])
    s_all = states[0] if bb == 1 else jnp.concatenate(states[:bb], axis=0)
    return s_all, states[bb]


def _gla_body(q, k, v, g, ra, wg, bg, gout, cmask, s_in, bb, ls, chained):
    rows = bb * ls
    sub = min(GLA_SUB, ls)
    n_sub = ls // sub
    hm = _head_masks()
    seg = lax.broadcasted_iota(jnp.int32, (rows, 1), 0) // ls

    x = jnp.dot(ra.astype(BF16), wg, preferred_element_type=F32) + bg
    log_a = (jnp.minimum(x, 0.0) - jnp.log1p(jnp.exp(-jnp.abs(x)))) * (1.0 / GLA_TAU)
    la_hi = log_a.astype(BF16)
    la_r = log_a - la_hi.astype(F32)
    la_mid = la_r.astype(BF16)
    la_lo = (la_r - la_mid.astype(F32)).astype(BF16)
    b3 = jnp.dot(cmask.astype(BF16), jnp.concatenate([la_hi, la_mid, la_lo], axis=1),
                 preferred_element_type=F32)
    bcum = b3[:, :LANES] + b3[:, LANES:2 * LANES] + b3[:, 2 * LANES:]

    def seg_row(offset):
        out = bcum[offset:offset + 1, :]
        for b in range(1, bb):
            out = jnp.where(seg == b, bcum[b * ls + offset:b * ls + offset + 1, :], out)
        return out

    bend = seg_row(ls - 1)
    bases = [None] + [seg_row(i * sub - 1) for i in range(1, n_sub)]
    sub_id = (lax.broadcasted_iota(jnp.int32, (rows, 1), 0) % ls) // sub
    bs = bcum
    for i in range(1, n_sub):
        bs = jnp.where(sub_id == i, bcum - bases[i], bs)

    qs = q * (DK ** -0.5)
    q_sub = qs * jnp.exp(bs)
    q_seq = qs * jnp.exp(bcum)
    k_end = k * jnp.exp(bend - bcum)

    a_parts, k_parts = [], []
    for i in range(n_sub):
        shift = -bcum if i == 0 else bases[i] - bcum
        k_parts.append((k * jnp.exp(jnp.minimum(shift, EXP_CLAMP))).astype(BF16))
        a_parts.append(jnp.where(sub_id == i, q_sub, 0.0) if n_sub > 1 else q_sub)
    a = a_parts[0] if n_sub == 1 else jnp.concatenate(a_parts, axis=1)
    kk = k_parts[0] if n_sub == 1 else jnp.concatenate(k_parts, axis=1)
    hmw = _head_masks(n_sub * LANES)
    a2 = jnp.concatenate([jnp.where(hmw[0], a, 0.0), jnp.where(hmw[1], a, 0.0)], axis=0).astype(BF16)
    att = _dot_nt(a2, kk)
    att = jnp.where(jnp.concatenate([cmask, cmask], axis=0) > 0.0, att, 0.0)

    vhs = [v[:, h * DV:(h + 1) * DV].astype(BF16) for h in range(2)]
    upd = (_dot_tn(_expand(jnp.where(hm[0], k_end, 0.0), bb, ls).astype(BF16), vhs[0])
           + _dot_tn(_expand(jnp.where(hm[1], k_end, 0.0), bb, ls).astype(BF16), vhs[1]))
    dec_parts = []
    for b in range(bb):
        last = b * ls + ls - 1
        dec_parts.append(jnp.exp(jnp.broadcast_to(bcum[last:last + 1, :], (LANES, LANES)).T))
    dec = dec_parts[0] if bb == 1 else jnp.concatenate(dec_parts, axis=0)
    if chained:
        s_all, s_out = _chain_states(s_in, dec, upd, bb)
    else:
        s_all, s_out = s_in, s_in * dec + upd

    s_bf = s_all.astype(BF16)
    outs = []
    for h in range(2):
        o = jnp.dot(att[h * rows:(h + 1) * rows].astype(BF16), vhs[h], preferred_element_type=F32)
        o = o + jnp.dot(_expand(jnp.where(hm[h], q_seq, 0.0), bb, ls).astype(BF16), s_bf,
                        preferred_element_type=F32)
        o = o * lax.rsqrt(jnp.mean(o * o, axis=-1, keepdims=True) + EPS) * gout
        outs.append(o * _silu(g[:, h * DV:(h + 1) * DV]))
    return jnp.concatenate(outs, axis=1), s_out


def _gla_kernel(cfg, *refs):
    bb, ls, chained, has_s0 = cfg
    (q_ref, k_ref, v_ref, g_ref, ra_ref, wg_ref, bg_ref, gout_ref, cm_ref) = refs[:9]
    if has_s0:
        s0_ref, o_ref, s_ref = refs[9:]
    else:
        o_ref, s_ref = refs[9:]

    @pl.when(pl.program_id(2) == 0)
    def _():
        s_ref[...] = s0_ref[...] if has_s0 else jnp.zeros(s_ref.shape, F32)

    s = s_ref[...].reshape(s_ref.shape[0] * LANES, LANES)
    o, s = _gla_body(q_ref[...], k_ref[...], v_ref[...], g_ref[...], ra_ref[...], wg_ref[...], bg_ref[...],
                     gout_ref[...], cm_ref[...], s, bb, ls, chained)
    o_ref[...] = o.astype(o_ref.dtype)
    s_ref[...] = s.reshape(s_ref.shape)


def _mixer_grid(n_seq, t, bb, ls, chained):
    tl = bb * ls
    if chained:
        steps = t // tl
        return tl, (n_seq, N_PAIR, steps), (lambda o, j, c: o * steps + c), 1
    assert t == ls
    return tl, (n_seq // bb, N_PAIR, 1), (lambda o, j, c: o), bb


def _gla_call(p, wg, bg, gout, s0, *, n_seq, t, bb, ls, chained):
    rows_total = n_seq * t
    tl, grid, row, sb = _mixer_grid(n_seq, t, bb, ls, chained)
    has_s0 = s0 is not None
    cm = _causal_mask(bb, ls)
    r = tl

    def col(width, base):
        return pl.BlockSpec((tl, width), lambda o, j, c: (row(o, j, c), base // width + j))

    const = lambda shape: pl.BlockSpec(shape, lambda o, j, c: (0,) * len(shape))
    st_spec = pl.BlockSpec((sb, None, LANES, LANES), lambda o, j, c: (o, j, 0, 0))
    in_specs = [col(LANES, COL_QA), col(LANES, COL_KA), col(2 * DV, COL_VA), col(2 * DV, COL_GA),
                pl.BlockSpec((tl, LANES), lambda o, j, c: (row(o, j, c), COL_RA // LANES)),
                pl.BlockSpec((LANES, LANES), lambda o, j, c: (0, j)),
                pl.BlockSpec((1, LANES), lambda o, j, c: (0, j)),
                const((1, DV)), const((r, r))]
    args = [p, p, p, p, p, wg, bg, gout, cm]
    if has_s0:
        in_specs.append(st_spec)
        args.append(s0)
    return pl.pallas_call(
        functools.partial(_gla_kernel, (bb, ls, chained, has_s0)),
        grid=grid, in_specs=in_specs,
        out_specs=[pl.BlockSpec((tl, 2 * DV), lambda o, j, c: (row(o, j, c), j)), st_spec],
        out_shape=[jax.ShapeDtypeStruct((rows_total, N_HEADS * DV), BF16),
                   jax.ShapeDtypeStruct((n_seq, N_PAIR, LANES, LANES), F32)],
        compiler_params=_cparams(("arbitrary", "arbitrary", "arbitrary")),
        name="gla_mixer",
    )(*args)


def _ret_tables(bb, ls, pos0):
    rows = bb * ls
    log_g = np.log1p(-np.exp2(-5.0 - np.arange(N_HEADS, dtype=np.float64)))
    t = np.arange(rows)
    n = t % ls
    same_seq = (t[:, None] // ls) == (t[None, :] // ls)
    diff = (n[:, None] - n[None, :]).astype(np.float64)
    causal = same_seq & (diff >= 0)
    dmask = np.where(causal[None], np.exp(np.maximum(diff, 0.0)[None] * log_g[:, None, None]), 0.0)
    lane_head = np.arange(N_PAIR * LANES) // DK
    q_dec = np.exp((n[:, None] + 1.0) * log_g[lane_head][None])
    k_dec = np.exp((ls - 1.0 - n)[:, None] * log_g[lane_head][None])
    c_dec = np.exp(ls * log_g[lane_head])[:, None] * np.ones((1, LANES))
    c_dec = np.tile(c_dec.reshape(N_PAIR, 1, LANES, LANES), (1, bb, 1, 1)).reshape(N_PAIR, bb * LANES, LANES)
    f32 = lambda a: jnp.asarray(a, dtype=F32)
    return f32(dmask), f32(q_dec), f32(k_dec), f32(c_dec)


def _rope_tables(t, pos0):
    half = DK // 2
    inv = ROPE_BASE ** (-np.arange(half, dtype=np.float64) / half)
    ang = (np.arange(t, dtype=np.float64) + pos0)[:, None] * inv[None]
    cos = np.tile(np.cos(ang), (1, 2 * LANES // DK))
    sin = np.tile(np.concatenate([-np.sin(ang), np.sin(ang)], axis=1), (1, LANES // DK))
    return jnp.asarray(cos, dtype=F32), jnp.asarray(sin, dtype=F32)


def _rope(x, cos, sin_signed):
    lane = lax.broadcasted_iota(jnp.int32, (1, LANES), 1) % DK
    swapped = jnp.where(lane < DK // 2, pltpu.roll(x, LANES - DK // 2, 1), pltpu.roll(x, DK // 2, 1))
    return x * cos + swapped * sin_signed


def _ret_body(q, k, v, g, cos, sin, dmask, q_dec, k_dec, c_dec, gn, s_in, bb, ls, chained):
    hm = _head_masks()
    qr = _rope(q, cos, sin)
    kr = _rope(k, cos, sin) * (DK ** -0.5)
    kr_bf = kr.astype(BF16)
    q_in = qr * q_dec
    k_end = kr * k_dec
    vhs = [v[:, h * DV:(h + 1) * DV].astype(BF16) for h in range(2)]
    upd = (_dot_tn(_expand(jnp.where(hm[0], k_end, 0.0), bb, ls).astype(BF16), vhs[0])
           + _dot_tn(_expand(jnp.where(hm[1], k_end, 0.0), bb, ls).astype(BF16), vhs[1]))
    if chained:
        s_all, s_out = _chain_states(s_in, c_dec, upd, bb)
    else:
        s_all, s_out = s_in, s_in * c_dec + upd
    s_bf = s_all.astype(BF16)
    outs = []
    for h in range(2):
        att = _dot_nt(jnp.where(hm[h], qr, 0.0).astype(BF16), kr_bf) * dmask[h]
        o = jnp.dot(att.astype(BF16), vhs[h], preferred_element_type=F32)
        o = o + jnp.dot(_expand(jnp.where(hm[h], q_in, 0.0), bb, ls).astype(BF16), s_bf,
                        preferred_element_type=F32)
        d = o - jnp.mean(o, axis=-1, keepdims=True)
        var = jnp.mean(d * d, axis=-1, keepdims=True)
        o = d * lax.rsqrt(var + EPS) * gn[:, h * DV:(h + 1) * DV]
        outs.append(o * _silu(g[:, h * DV:(h + 1) * DV]))
    return jnp.concatenate(outs, axis=1), s_out


def _ret_kernel(cfg, *refs):
    bb, ls, chained, has_s0 = cfg
    (q_ref, k_ref, v_ref, g_ref, cos_ref, sin_ref, dm_ref, qd_ref, kd_ref, cd_ref, gn_ref) = refs[:11]
    if has_s0:
        s0_ref, o_ref, s_ref = refs[11:]
    else:
        o_ref, s_ref = refs[11:]

    @pl.when(pl.program_id(2) == 0)
    def _():
        s_ref[...] = s0_ref[...] if has_s0 else jnp.zeros(s_ref.shape, F32)

    s = s_ref[...].reshape(s_ref.shape[0] * LANES, LANES)
    o, s = _ret_body(q_ref[...], k_ref[...], v_ref[...], g_ref[...], cos_ref[...], sin_ref[...], dm_ref[...],
                     qd_ref[...], kd_ref[...], cd_ref[...], gn_ref[...], s, bb, ls, chained)
    o_ref[...] = o.astype(o_ref.dtype)
    s_ref[...] = s.reshape(s_ref.shape)


def _ret_call(p, gn, s0, *, n_seq, t, bb, ls, chained, pos0):
    rows_total = n_seq * t
    tl, grid, row, sb = _mixer_grid(n_seq, t, bb, ls, chained)
    has_s0 = s0 is not None
    dmask, q_dec, k_dec, c_dec = _ret_tables(bb, ls, pos0)
    cos, sin = _rope_tables(t, pos0)
    r = tl
    if chained:
        rope_spec = pl.BlockSpec((tl, LANES), lambda o, j, c: (c, 0))
    else:
        cos, sin = jnp.tile(cos, (bb, 1)), jnp.tile(sin, (bb, 1))
        rope_spec = pl.BlockSpec((tl, LANES), lambda o, j, c: (0, 0))

    def col(width, base):
        return pl.BlockSpec((tl, width), lambda o, j, c: (row(o, j, c), base // width + j))

    st_spec = pl.BlockSpec((sb, None, LANES, LANES), lambda o, j, c: (o, j, 0, 0))
    in_specs = [col(LANES, COL_QC), col(LANES, COL_KC), col(2 * DV, COL_VC), col(2 * DV, COL_GC),
                rope_spec, rope_spec,
                pl.BlockSpec((2, r, r), lambda o, j, c: (j, 0, 0)),
                pl.BlockSpec((r, LANES), lambda o, j, c: (0, j)),
                pl.BlockSpec((r, LANES), lambda o, j, c: (0, j)),
                pl.BlockSpec((None, bb * LANES, LANES), lambda o, j, c: (j, 0, 0)),
                pl.BlockSpec((1, 2 * DV), lambda o, j, c: (0, j))]
    args = [p, p, p, p, cos, sin, dmask, q_dec, k_dec, c_dec, gn]
    if has_s0:
        in_specs.append(st_spec)
        args.append(s0)
    return pl.pallas_call(
        functools.partial(_ret_kernel, (bb, ls, chained, has_s0)),
        grid=grid, in_specs=in_specs,
        out_specs=[pl.BlockSpec((tl, 2 * DV), lambda o, j, c: (row(o, j, c), j)), st_spec],
        out_shape=[jax.ShapeDtypeStruct((rows_total, N_HEADS * DV), BF16),
                   jax.ShapeDtypeStruct((n_seq, N_PAIR, LANES, LANES), F32)],
        compiler_params=_cparams(("arbitrary", "arbitrary", "arbitrary")),
        name="ret_mixer",
    )(*args)


def _gelu(x):
    return 0.5 * x * (1.0 + jnp.tanh(0.7978845608028654 * (x + 0.044715 * x * x * x)))


def _cmlp_kernel(n_tiles, want_v, u_ref, v_ref, w_ref, bias_ref, lg_ref, lb_ref, *out_refs):
    o_ref = out_refs[0]
    for ti in range(n_tiles):
        sl = slice(ti * CHUNK_B, (ti + 1) * CHUNK_B)
        u = _gelu(u_ref[sl, :])
        vv = _gelu(v_ref[sl, :])
        vn_parts, z_parts = [], []
        for gi in range(G_B):
            vg = vv[:, gi * CG_B:(gi + 1) * CG_B]
            d = vg - jnp.mean(vg, axis=-1, keepdims=True)
            vn = d * lax.rsqrt(jnp.mean(d * d, axis=-1, keepdims=True) + EPS)
            vn = vn * lg_ref[:, gi * CG_B:(gi + 1) * CG_B] + lb_ref[:, gi * CG_B:(gi + 1) * CG_B]
            vn_parts.append(vn)
            z_parts.append(jnp.dot(w_ref[gi], vn.astype(BF16), preferred_element_type=F32))
        z = jnp.concatenate(z_parts, axis=1) + bias_ref[...]
        o_ref[sl, :] = (u * z).astype(o_ref.dtype)
        if want_v:
            out_refs[1][sl, :] = jnp.concatenate(vn_parts, axis=1)


def _cmlp_call(p, w_mix, bias_tile, ln_g, ln_b, *, n_tiles, want_v):
    rows = p.shape[0]
    tl = n_tiles * CHUNK_B
    const = lambda shape: pl.BlockSpec(shape, lambda i: (0,) * len(shape))
    out_specs = [pl.BlockSpec((tl, W_B), lambda i: (i, 0))]
    out_shape = [jax.ShapeDtypeStruct((rows, W_B), BF16)]
    if want_v:
        out_specs.append(pl.BlockSpec((tl, W_B), lambda i: (i, 0)))
        out_shape.append(jax.ShapeDtypeStruct((rows, W_B), F32))
    out = pl.pallas_call(
        functools.partial(_cmlp_kernel, n_tiles, want_v),
        grid=(rows // tl,),
        in_specs=[pl.BlockSpec((tl, W_B), lambda i: (i, COL_UB // W_B)),
                  pl.BlockSpec((tl, W_B), lambda i: (i, COL_VB // W_B)),
                  const((G_B, CHUNK_B, CHUNK_B)), const((CHUNK_B, W_B)), const((1, W_B)), const((1, W_B))],
        out_specs=out_specs, out_shape=out_shape,
        compiler_params=_cparams(("arbitrary",)),
        name="cmlp_mixer",
    )(p, p, w_mix, bias_tile, ln_g, ln_b)
    return (out[0], out[1]) if want_v else (out[0], None)


def _outproj_kernel(oa_ref, ob_ref, oc_ref, w_ref, x_ref, g1_ref, gn_ref, sc_ref, sh_ref,
                    wr_ref, br_ref, xo_ref, h_ref, lg_ref):
    wa = N_HEADS * DV
    acc = jnp.dot(oa_ref[...], w_ref[0:wa, :], preferred_element_type=F32)
    acc = acc + jnp.dot(ob_ref[...], w_ref[wa:wa + W_B, :], preferred_element_type=F32)
    acc = acc + jnp.dot(oc_ref[...], w_ref[wa + W_B:, :], preferred_element_type=F32)
    x = x_ref[...] + g1_ref[...] * acc
    xo_ref[...] = x
    h = _rms_mod(x, gn_ref[...], sc_ref[...], sh_ref[...])
    h_ref[...] = _pack_bf16_pairs(h, h.shape[1] // 2)
    hi = h.astype(BF16)
    lo = (h - hi.astype(F32)).astype(BF16)
    r1 = jnp.dot(hi, wr_ref[...], preferred_element_type=F32)
    r2 = jnp.dot(lo, wr_ref[:, :LANES], preferred_element_type=F32)
    lg_ref[...] = r1[:, :LANES] + r1[:, LANES:] + r2 + br_ref[...]


def _outproj_call(oa, ob, oc, w_out, layer, x, gate1, gnorm, scale, shift, w_router, b_router, *, tm,
                  tiles_per_group):
    rows, d = x.shape
    rspec = lambda width: pl.BlockSpec((tm, width), lambda i: (i, 0))
    const = lambda shape: pl.BlockSpec(shape, lambda i: (0,) * len(shape))
    ms = lambda m: _mod_spec(m, tiles_per_group)
    return pl.pallas_call(
        _outproj_kernel,
        grid=(rows // tm,),
        in_specs=[rspec(oa.shape[1]), rspec(ob.shape[1]), rspec(oc.shape[1]),
                  pl.BlockSpec((None,) + w_out.shape[1:], lambda i: (layer, 0, 0)),
                  rspec(d), ms(gate1), const((1, d)), ms(scale), ms(shift),
                  const(w_router.shape), const(b_router.shape)],
        out_specs=[rspec(d), rspec(d // 2), rspec(LANES)],
        out_shape=[jax.ShapeDtypeStruct((rows, d), F32), jax.ShapeDtypeStruct((rows, d // 2), jnp.uint32),
                   jax.ShapeDtypeStruct((rows, LANES), F32)],
        compiler_params=_cparams(("arbitrary",)),
        name="outproj_norm_router",
    )(oa, ob, oc, w_out, x, gate1, gnorm, scale, shift, w_router, b_router)


def _moe_in_kernel(se_ref, nb_ref, so_ref, x_ref, wg_ref, wu_ref, bg_ref, bu_ref, h_ref, x_s, wg_s, wu_s):
    nb = nb_ref[pl.program_id(0)]

    @pl.when(pl.program_id(1) == 0)
    def _():
        for rb in range(MOE_SLAB // MOE_SUB):
            sl = slice(rb * MOE_SUB, (rb + 1) * MOE_SUB)

            @pl.when(rb < nb)
            def _():
                x_s[sl, :] = _unpack_bf16_pairs(x_ref[sl, :], x_ref.shape[1]).astype(BF16)

    @pl.when(nb > 0)
    def _():
        wg_s[...] = wg_ref[...].astype(BF16)
        wu_s[...] = wu_ref[...].astype(BF16)

    @pl.when(nb == 0)
    def _():
        h_ref[...] = jnp.zeros(h_ref.shape, h_ref.dtype)

    for n in range(1, MOE_SLAB // MOE_SUB + 1):
        rows = n * MOE_SUB

        @pl.when(nb == n)
        def _():
            for rb in range(n):
                sl = slice(rb * MOE_SUB, (rb + 1) * MOE_SUB)
                xb = x_s[sl, :]
                gt = jnp.dot(xb, wg_s[...], preferred_element_type=F32) + bg_ref[...]
                up = jnp.dot(xb, wu_s[...], preferred_element_type=F32) + bu_ref[...]
                gt = jnp.minimum(gt, SWIGLU_LIMIT)
                up = jnp.clip(up, -SWIGLU_LIMIT, SWIGLU_LIMIT)
                glu = gt * jax.nn.sigmoid(SWIGLU_ALPHA * gt)
                h_ref[sl, :] = ((up + 1.0) * glu).astype(h_ref.dtype)
            if rows < MOE_SLAB:
                h_ref[rows:, :] = jnp.zeros((MOE_SLAB - rows, h_ref.shape[1]), h_ref.dtype)


def _moe_out_kernel(se_ref, nb_ref, so_ref, h_ref, w_ref, b_ref, y_ref, w_s):
    nb = nb_ref[pl.program_id(0)]

    @pl.when(nb > 0)
    def _():
        w_s[...] = w_ref[...].astype(BF16)

    @pl.when(nb == 0)
    def _():
        y_ref[...] = jnp.zeros(y_ref.shape, y_ref.dtype)

    for n in range(1, MOE_SLAB // MOE_SUB + 1):
        rows = n * MOE_SUB

        @pl.when(nb == n)
        def _():
            y = jnp.dot(h_ref[0:rows, :], w_s[...], preferred_element_type=F32) + b_ref[...]
            y_ref[0:rows, :] = _pack_bf16_pairs(y, MOE_TN2 // 2)
            if rows < MOE_SLAB:
                y_ref[rows:, :] = jnp.zeros((MOE_SLAB - rows, y_ref.shape[1]), y_ref.dtype)


def _moe_calls(x_rows, n_used, slab_e, slab_nb, slab_off, w_in, b_in, w_out, b_out, layer):
    n_slab = slab_e.shape[0]
    r = MOE_SLAB
    dh = x_rows.shape[1]
    d = 2 * dh
    nt1 = D_FF // MOE_TN1
    nt2 = d // MOE_TN2
    up_off = D_FF // MOE_TN1

    def wspec(k, tn, last, off=0):
        return pl.BlockSpec((None, None, k, tn),
                            lambda s, j, se, nb, so: (layer, se[s], 0, off + jnp.where(nb[s] > 0, j, last)))

    del n_used
    h = pl.pallas_call(
        _moe_in_kernel,
        grid_spec=pltpu.PrefetchScalarGridSpec(
            num_scalar_prefetch=3, grid=(n_slab, nt1),
            in_specs=[
                pl.BlockSpec((pl.Element(r), pl.Element(dh)), lambda s, j, se, nb, so: (so[s] * MOE_SUB, 0)),
                wspec(d, MOE_TN1, nt1 - 1), wspec(d, MOE_TN1, nt1 - 1, up_off),
                wspec(1, MOE_TN1, nt1 - 1), wspec(1, MOE_TN1, nt1 - 1, up_off),
            ],
            out_specs=pl.BlockSpec((None, r, MOE_TN1), lambda s, j, se, nb, so: (s, 0, j)),
            scratch_shapes=[pltpu.VMEM((r, d), BF16), pltpu.VMEM((d, MOE_TN1), BF16),
                            pltpu.VMEM((d, MOE_TN1), BF16)]),
        out_shape=jax.ShapeDtypeStruct((n_slab, r, D_FF), BF16),
        compiler_params=_cparams(("arbitrary", "arbitrary")),
        name="moe_glu",
    )(slab_e, slab_nb, slab_off, x_rows, w_in, w_in, b_in, b_in)

    y = pl.pallas_call(
        _moe_out_kernel,
        grid_spec=pltpu.PrefetchScalarGridSpec(
            num_scalar_prefetch=3, grid=(n_slab, nt2),
            in_specs=[
                pl.BlockSpec((None, r, D_FF), lambda s, j, se, nb, so: (s, 0, 0)),
                wspec(D_FF, MOE_TN2, nt2 - 1), wspec(1, MOE_TN2, nt2 - 1),
            ],
            out_specs=pl.BlockSpec((None, r, MOE_TN2 // 2), lambda s, j, se, nb, so: (s, 0, j)),
            scratch_shapes=[pltpu.VMEM((D_FF, MOE_TN2), BF16)]),
        out_shape=jax.ShapeDtypeStruct((n_slab, r, dh), jnp.uint32),
        compiler_params=_cparams(("arbitrary", "arbitrary")),
        name="moe_down",
    )(slab_e, slab_nb, slab_off, h, w_out, b_out)
    return y


def _rows(a, idx):
    return a.at[idx].get(mode="promise_in_bounds")


def _count_le(bounds, x):
    return jnp.sum((bounds <= x[..., None]).astype(jnp.int32), axis=-1)


def _n_slabs(n_assign):
    return N_EXPERTS + (n_assign + N_EXPERTS * (MOE_SUB - 1)) // MOE_SLAB


def _route(logits, n_slab):
    m = logits.shape[0]
    top_val, top_idx = lax.top_k(logits, TOP_K)
    gates = jax.nn.softmax(top_val, axis=-1)
    a = m * TOP_K
    flat_e = top_idx.reshape(a).astype(jnp.int32)
    ar = jnp.arange(a, dtype=jnp.int32)
    sorted_e, order = lax.sort((flat_e, ar), num_keys=1, is_stable=True)
    counts = jnp.bincount(flat_e, length=N_EXPERTS).astype(jnp.int32)
    start = jnp.cumsum(counts) - counts
    rank = ar - start[sorted_e]
    padded = (counts + MOE_SUB - 1) // MOE_SUB * MOE_SUB
    nslab_e = (padded + MOE_SLAB - 1) // MOE_SLAB
    slab_end = jnp.cumsum(nslab_e)
    slab_start = slab_end - nslab_e
    dest = (slab_start[sorted_e] + rank // MOE_SLAB) * MOE_SLAB + rank % MOE_SLAB
    _, pos = lax.sort((order, dest), num_keys=1)
    pos = pos.reshape(m, TOP_K)
    n_used = slab_end[-1]
    sid = jnp.arange(n_slab, dtype=jnp.int32)
    sid_eff = jnp.minimum(sid, n_used - 1)
    slab_e = jnp.minimum(_count_le(slab_end, sid_eff), N_EXPERTS - 1)
    local = sid_eff - slab_start[slab_e]
    slab_nb = jnp.clip(padded[slab_e] - local * MOE_SLAB, 0, MOE_SLAB) // MOE_SUB
    slab_nb = jnp.where(sid < n_used, slab_nb, 0).astype(jnp.int32)
    cend = jnp.cumsum(padded)
    cstart = cend - padded
    slab_off = ((cstart[slab_e] + local * MOE_SLAB) // MOE_SUB).astype(jnp.int32)
    n_rows = a + N_EXPERTS * MOE_SUB + MOE_SLAB
    rid = jnp.arange(n_rows, dtype=jnp.int32)
    e_r = jnp.minimum(_count_le(cend, rid), N_EXPERTS - 1)
    rk = rid - cstart[e_r]
    src = jnp.clip(start[e_r] + rk, 0, a - 1)
    row_tok = jnp.where(rk < counts[e_r], _rows(order, src) // TOP_K, rid % m)
    return gates, pos, row_tok, n_used.astype(jnp.int32), slab_e, slab_nb, slab_off


def _moe(h2u, logits, w_in, b_in, w_out, b_out, layer):
    m, dh = h2u.shape
    n_slab = _n_slabs(m * TOP_K)
    gates, pos, row_tok, n_used, slab_e, slab_nb, slab_off = _route(logits, n_slab)
    x_rows = _rows(h2u, row_tok)
    y = _moe_calls(x_rows, n_used, slab_e, slab_nb, slab_off, w_in, b_in.reshape(b_in.shape[0], N_EXPERTS, 1, -1),
                   w_out, b_out.reshape(b_out.shape[0], N_EXPERTS, 1, -1), layer)
    y4 = _rows(y.reshape(n_slab * MOE_SLAB, dh), pos.T.reshape(-1))
    return y4.reshape(TOP_K, m, dh), gates


def _prep_w_in(w_in):
    depth, d, _ = w_in.shape
    w = w_in.astype(BF16)
    pad = jnp.zeros((depth, d, D_P - COL_RA - GLA_LOWRANK), BF16)
    return jnp.concatenate([w[:, :, _O_UB:_O_QC], w[:, :, :_O_RA], w[:, :, _O_QC:_O_END],
                            w[:, :, _O_RA:_O_UB], pad], axis=-1)


def _prep_cmlp(w_spatial, b_spatial, t):
    tri = jnp.tril(jnp.ones((CHUNK_B, CHUNK_B), bool))
    w = jnp.where(tri[None, None], w_spatial, 0.0)
    if t < CHUNK_B:
        reps = CHUNK_B // t
        eye = jnp.eye(reps, dtype=w.dtype)
        w = jnp.einsum("ab,lgij->lgaibj", eye, w[:, :, :t, :t]).reshape(w.shape[0], G_B, CHUNK_B, CHUNK_B)
        b = jnp.tile(b_spatial[:, :, :t], (1, 1, reps))
    else:
        b = b_spatial
    bias = jnp.repeat(jnp.swapaxes(b, 1, 2), CG_B, axis=2)
    return w.astype(BF16), bias.astype(F32)


def kernel(x_prompt, x_sample, c_prompt, c_sample, state_gla, state_ret, w_ada, b_ada, g_norm1,
           g_norm2, w_in, w_gla_gate, b_gla_gate, g_gla_out, ln_g_cmlp, ln_b_cmlp, w_spatial,
           b_spatial, g_ret_out, w_out, w_router, b_router, w_e_in, b_e_in, w_e_out, b_e_out, g_final):
    bp, tp, d = x_prompt.shape
    bs, ts, _ = x_sample.shape
    rows_p, rows_s = bp * tp, bs * ts
    tmn_p, tmn_s = 256, 256
    tmo_p, tmo_s = 512, 256
    tm_in, tn_in = 1024, 1152
    dec_bb = LANES // ts

    n_c = bp + bs
    n_c_pad = -(-n_c // SUBLANES) * SUBLANES
    c_all = jnp.concatenate([c_prompt, c_sample, jnp.zeros((n_c_pad - n_c, d), F32)], axis=0)
    mod = _ada_call(c_all, w_ada, b_ada).reshape(DEPTH, n_c_pad, N_MOD, d)
    mod_p = jnp.transpose(mod[:, :bp], (0, 2, 1, 3)).reshape(DEPTH, N_MOD, bp, 1, d)
    mod_s = jnp.repeat(jnp.transpose(mod[:, bp:n_c], (0, 2, 1, 3)), ts, axis=2)

    w_in_r = _prep_w_in(w_in)
    w_out_bf = w_out.astype(BF16)
    wg_pad = jnp.pad(w_gla_gate, ((0, 0), (0, LANES - GLA_LOWRANK), (0, 0))).astype(BF16)
    w_router_pad = jnp.pad(w_router, ((0, 0), (0, 0), (0, LANES - N_EXPERTS)))
    w_router_hi = w_router_pad.astype(BF16)
    w_router_lo = (w_router_pad - w_router_hi.astype(F32)).astype(BF16)
    w_router_2 = jnp.concatenate([w_router_hi, w_router_lo], axis=-1)
    b_router_pad = jnp.pad(b_router, ((0, 0), (0, LANES - N_EXPERTS))).reshape(DEPTH, 1, LANES)
    wmix_p, bias_p = _prep_cmlp(w_spatial, b_spatial, CHUNK_B)
    wmix_s, bias_s = _prep_cmlp(w_spatial, b_spatial, ts)

    xp = x_prompt.reshape(rows_p, d)
    xs = x_sample.reshape(rows_s, d)
    st_gla = state_gla.reshape(DEPTH, bs, N_PAIR, LANES, LANES)
    st_ret = state_ret.reshape(DEPTH, bs, N_PAIR, LANES, LANES)
    y4 = gates = None
    outs = {k: [] for k in ("gla_p", "ret_p", "gla_s", "ret_s", "v_s")}

    for l in range(DEPTH):
        row = lambda a: a[l].reshape(1, -1)
        mp = lambda i: mod_p[l, i]
        msf = lambda i, tm: mod_s[l, i].reshape(rows_s // tm, tm, d)
        g2p = mod_p[l - 1, 5] if l > 0 else None
        g2s = mod_s[l - 1, 5].reshape(rows_s // tmn_s, tmn_s, d) if l > 0 else None

        xp, hp = _resid_norm_call(xp, y4, gates, g2p, row(g_norm1), mp(1), mp(0), tm=tmn_p,
                                  tiles_per_group=tp // tmn_p, row_off=0, final=False)
        xs, hs = _resid_norm_call(xs, y4, gates, g2s, row(g_norm1), msf(1, tmn_s), msf(0, tmn_s), tm=tmn_s,
                                  tiles_per_group=1, row_off=rows_p, final=False)
        pp = _inproj_call(hp, w_in_r, l, tm=tm_in, tn=tn_in)
        ps = _inproj_call(hs, w_in_r, l, tm=tm_in, tn=tn_in)

        bg = row(b_gla_gate)
        gout = row(g_gla_out)
        gret = row(g_ret_out)
        oa_p, sa_p = _gla_call(pp, wg_pad[l], bg, gout, None, n_seq=bp, t=tp, bb=4, ls=GLA_CHUNK, chained=True)
        oa_s, sa_s = _gla_call(ps, wg_pad[l], bg, gout, st_gla[l], n_seq=bs, t=ts, bb=dec_bb, ls=ts,
                               chained=False)
        oc_p, sc_p = _ret_call(pp, gret, None, n_seq=bp, t=tp, bb=2, ls=RET_CHUNK, chained=True, pos0=0)
        oc_s, sc_s = _ret_call(ps, gret, st_ret[l], n_seq=bs, t=ts, bb=dec_bb, ls=ts, chained=False,
                               pos0=PAST_LEN)
        ob_p, _ = _cmlp_call(pp, wmix_p[l], bias_p[l], row(ln_g_cmlp), row(ln_b_cmlp), n_tiles=4, want_v=False)
        ob_s, v_s = _cmlp_call(ps, wmix_s[l], bias_s[l], row(ln_g_cmlp), row(ln_b_cmlp), n_tiles=4, want_v=True)

        xp, h2p, lgp = _outproj_call(oa_p, ob_p, oc_p, w_out_bf, l, xp, mp(2), row(g_norm2), mp(4), mp(3),
                                     w_router_2[l], b_router_pad[l], tm=tmo_p, tiles_per_group=tp // tmo_p)
        xs, h2s, lgs = _outproj_call(oa_s, ob_s, oc_s, w_out_bf, l, xs, msf(2, tmo_s), row(g_norm2),
                                     msf(4, tmo_s), msf(3, tmo_s), w_router_2[l], b_router_pad[l],
                                     tm=tmo_s, tiles_per_group=1)

        h2 = jnp.concatenate([h2p, h2s], axis=0)
        logits = jnp.concatenate([lgp, lgs], axis=0)[:, :N_EXPERTS]
        y4, gates = _moe(h2, logits, w_e_in, b_e_in, w_e_out, b_e_out, l)

        outs["gla_p"].append(sa_p.reshape(bp, N_HEADS, DK, DV))
        outs["ret_p"].append(sc_p.reshape(bp, N_HEADS, DK, DV))
        outs["gla_s"].append(sa_s.reshape(bs, N_HEADS, DK, DV))
        outs["ret_s"].append(sc_s.reshape(bs, N_HEADS, DK, DV))
        outs["v_s"].append(v_s.reshape(bs, ts, W_B))

    yp = _resid_norm_call(xp, y4, gates, mod_p[DEPTH - 1, 5], g_final.reshape(1, d), None, None, tm=tmn_p,
                          tiles_per_group=tp // tmn_p, row_off=0, final=True)
    ys = _resid_norm_call(xs, y4, gates, mod_s[DEPTH - 1, 5].reshape(rows_s // tmn_s, tmn_s, d),
                          g_final.reshape(1, d), None, None, tm=tmn_s, tiles_per_group=1, row_off=rows_p,
                          final=True)
    return (yp.reshape(bp, tp, d), ys.reshape(bs, ts, d), jnp.stack(outs["gla_p"]), jnp.stack(outs["ret_p"]),
            jnp.stack(outs["gla_s"]), jnp.stack(outs["ret_s"]), jnp.stack(outs["v_s"]))
```
